```python
import jax
import jax.numpy as jnp
from jax import lax
import numpy as np

D_MODEL = 2048
BATCH = 2
SEQ = 8192
DEPTH = 4

N_EVEN = (DEPTH + 1) // 2
N_ODD = DEPTH // 2

FOX_HEADS = 8
FOX_HEAD_DIM = 128
FOX_WIDTH = FOX_HEADS * FOX_HEAD_DIM
Q_BLOCK = 128

RWKV_HEAD_DIM = 64
RWKV_WIDTH = D_MODEL - FOX_WIDTH
RWKV_HEADS = RWKV_WIDTH // RWKV_HEAD_DIM
DECAY_RANK = 96
ICLR_RANK = 96
GATE_RANK = 256
GN_EPS = 64e-5

FOX_COLS = 3 * FOX_WIDTH + FOX_HEADS
RWKV_COLS = 3 * RWKV_WIDTH + DECAY_RANK + ICLR_RANK + GATE_RANK
IN_COLS = FOX_COLS + RWKV_COLS

POOL_WINDOWS = (2, 4, 8, 16)
POOL_GROUPS = len(POOL_WINDOWS)
POOL_GROUP_DIM = D_MODEL // POOL_GROUPS

D_FF = 5632
CONV_WIDTH = 3
RMS_EPS = 1e-6

kernel_name = 'fox_rwkv7_pool_hybrid'


def rmsnorm(x, g):
    xf = x.astype(jnp.float32)
    y = xf * lax.rsqrt(jnp.mean(xf * xf, axis=-1, keepdims=True) + RMS_EPS)
    return (y * g.astype(jnp.float32)).astype(x.dtype)


def token_shift(x):
    return jnp.pad(x, ((0, 0), (1, 0), (0, 0)))[:, :-1]


def forgetting_attention(q, k, v, log_f):
    B, S, H, Dh = q.shape
    nb = S // Q_BLOCK
    c = jnp.cumsum(log_f, axis=1)
    c_keys = c.transpose(0, 2, 1)
    q_blocks = q.reshape(B, nb, Q_BLOCK, H, Dh).transpose(1, 0, 3, 2, 4)
    c_blocks = c.reshape(B, nb, Q_BLOCK, H).transpose(1, 0, 3, 2)
    k_pos = jnp.arange(S)
    scale = Dh ** -0.5

    def one_block(args):
        i, q_i, c_i = args
        s = jnp.einsum('bhqd,bkhd->bhqk', q_i, k, preferred_element_type=jnp.float32) * scale
        s = s + (c_i[..., :, None] - c_keys[..., None, :])
        q_pos = i * Q_BLOCK + jnp.arange(Q_BLOCK)
        causal = k_pos[None, :] <= q_pos[:, None]
        s = jnp.where(causal, s, -jnp.inf)
        p = jax.nn.softmax(s, axis=-1)
        return jnp.einsum('bhqk,bkhd->bqhd', p.astype(v.dtype), v)

    out = lax.map(one_block, (jnp.arange(nb), q_blocks, c_blocks))
    return out.transpose(1, 0, 2, 3, 4).reshape(B, S, H * Dh)


def rwkv7_scan(r, log_w, k, v, a, kk):
    B, S, H, N = r.shape

    def step(state, inp):
        r_t, lw_t, k_t, v_t, a_t, kk_t = inp
        s_kk = jnp.einsum('bhvk,bhk->bhv', state, kk_t)
        state = (state * jnp.exp(lw_t)[:, :, None, :]
                 - s_kk[..., None] * (kk_t * a_t)[:, :, None, :]
                 + v_t[..., None] * k_t[:, :, None, :])
        y_t = jnp.einsum('bhvk,bhk->bhv', state, r_t)
        return state, y_t

    xs = tuple(t.transpose(1, 0, 2, 3) for t in (r, log_w, k, v, a, kk))
    state0 = jnp.zeros((B, H, N, N), jnp.float32)
    _, y = lax.scan(step, state0, xs)
    return y.transpose(1, 0, 2, 3)


def fox_rwkv_layer(x, norm_g, w_in, b_f, shift_mu, w0, w_decay_up, a0, w_iclr_up,
                   w_gate_up, k_k, k_a, r_k, gn_w, gn_b, w_out):
    B, S, _ = x.shape
    f32 = jnp.float32
    h = rmsnorm(x, norm_g)
    p = h @ w_in
    fa = p[..., :FOX_COLS]
    rw = p[..., FOX_COLS:].astype(f32)

    q, k, v, f_logit = jnp.split(fa, [FOX_WIDTH, 2 * FOX_WIDTH, 3 * FOX_WIDTH], axis=-1)
    log_f = jax.nn.log_sigmoid(f_logit.astype(f32) + b_f.astype(f32))
    hs = (B, S, FOX_HEADS, FOX_HEAD_DIM)
    o_a = forgetting_attention(q.reshape(hs), k.reshape(hs), v.reshape(hs), log_f)

    rw = rw + (token_shift(rw) - rw) * shift_mu.astype(f32)
    r, kb, vb, wd, ad, gd = jnp.split(
        rw, [RWKV_WIDTH, 2 * RWKV_WIDTH, 3 * RWKV_WIDTH,
             3 * RWKV_WIDTH + DECAY_RANK, 3 * RWKV_WIDTH + DECAY_RANK + ICLR_RANK], axis=-1)
    log_w = -jnp.exp(-jax.nn.softplus(-(w0.astype(f32) + jnp.tanh(wd) @ w_decay_up.astype(f32))) - 0.5)
    a = jax.nn.sigmoid(a0.astype(f32) + ad @ w_iclr_up.astype(f32))
    g = jax.nn.sigmoid(gd) @ w_gate_up.astype(f32)
    hr = (B, S, RWKV_HEADS, RWKV_HEAD_DIM)
    kk = (kb * k_k.astype(f32)).reshape(hr)
    kk = kk / jnp.maximum(jnp.sqrt(jnp.sum(kk * kk, axis=-1, keepdims=True)), 1e-12)
    kb = kb * (1.0 + (a - 1.0) * k_a.astype(f32))
    r_h, k_h, v_h = r.reshape(hr), kb.reshape(hr), vb.reshape(hr)
    y = rwkv7_scan(r_h, log_w.reshape(hr), k_h, v_h, a.reshape(hr), kk)
    mu = jnp.mean(y, axis=-1, keepdims=True)
    var = jnp.mean(jnp.square(y - mu), axis=-1, keepdims=True)
    y = ((y - mu) * lax.rsqrt(var + GN_EPS) * gn_w.astype(f32).reshape(RWKV_HEADS, RWKV_HEAD_DIM)
         + gn_b.astype(f32).reshape(RWKV_HEADS, RWKV_HEAD_DIM))
    y = y + jnp.sum(r_h * k_h * r_k.astype(f32), axis=-1, keepdims=True) * v_h
    o_b = (y.reshape(B, S, RWKV_WIDTH) * g).astype(x.dtype)

    mixed = jnp.concatenate([o_a.astype(x.dtype), o_b], axis=-1)
    return x + mixed @ w_out


def pool_layer(x, norm_g, w_pool, pool_scale):
    B, S, D = x.shape
    h = rmsnorm(x, norm_g).astype(jnp.float32).reshape(B, S, POOL_GROUPS, POOL_GROUP_DIM)
    cs = jnp.pad(jnp.cumsum(h, axis=1), ((0, 0), (1, 0), (0, 0), (0, 0)))
    t = jnp.arange(S)
    win = jnp.array(POOL_WINDOWS, dtype=jnp.int32)
    lo = jnp.maximum(t[:, None] + 1 - win[None, :], 0)
    grp = jnp.arange(POOL_GROUPS)[None, :]
    window_sum = cs[:, 1:] - cs[:, lo, grp]
    count = jnp.minimum(t[:, None] + 1, win[None, :]).astype(jnp.float32)
    pooled = window_sum / count[None, :, :, None] - h
    y = jnp.einsum('bsgc,gcd->bsgd', pooled.astype(x.dtype), w_pool).reshape(B, S, D)
    return x + y * pool_scale


def conv_ffn(x, norm_g, w_up, conv_w, conv_b, w_down):
    h = rmsnorm(x, norm_g)
    u = h @ w_up
    C = u.shape[-1]
    u = lax.conv_general_dilated(
        u, conv_w[:, None, :].astype(u.dtype), window_strides=(1,),
        padding=[(CONV_WIDTH - 1, 0)], dimension_numbers=('NWC', 'WIO', 'NWC'),
        feature_group_count=C) + conv_b
    gate, val = jnp.split(u, 2, axis=-1)
    return x + (jax.nn.silu(gate) * val) @ w_down


def setup_inputs(seed: int = 0) -> dict:
    key = jax.random.key(seed)
    ks = iter(jax.random.split(key, 32))
    f32 = jnp.float32

    def nrm(shape, scale):
        return jax.random.normal(next(ks), shape, f32) * scale

    def uni(shape, lo, hi):
        return jax.random.uniform(next(ks), shape, f32, lo, hi)

    E, O, L, D = N_EVEN, N_ODD, DEPTH, D_MODEL
    return {
        'x': nrm((BATCH, SEQ, D), 1.0),
        'mix_norm': 1.0 + nrm((E, D), 0.02),
        'w_in': nrm((E, D, IN_COLS), D ** -0.5),
        'b_f': uni((E, FOX_HEADS), 1.0, 6.0),
        'shift_mu': uni((E, RWKV_COLS), 0.0, 1.0),
        'w0': uni((E, RWKV_WIDTH), -6.0, -1.0),
        'w_decay_up': nrm((E, DECAY_RANK, RWKV_WIDTH), 0.1 * DECAY_RANK ** -0.5),
        'a0': nrm((E, RWKV_WIDTH), 0.1),
        'w_iclr_up': nrm((E, ICLR_RANK, RWKV_WIDTH), ICLR_RANK ** -0.5),
        'w_gate_up': nrm((E, GATE_RANK, RWKV_WIDTH), GATE_RANK ** -0.5),
        'k_k': 0.85 + nrm((E, RWKV_WIDTH), 0.02),
        'k_a': 1.0 + nrm((E, RWKV_WIDTH), 0.02),
        'r_k': nrm((E, RWKV_HEADS, RWKV_HEAD_DIM), 0.1),
        'gn_w': 1.0 + nrm((E, RWKV_WIDTH), 0.02),
        'gn_b': nrm((E, RWKV_WIDTH), 0.02),
        'w_out': nrm((E, D, D), D ** -0.5),
        'pool_norm': 1.0 + nrm((O, D), 0.02),
        'w_pool': nrm((O, POOL_GROUPS, POOL_GROUP_DIM, POOL_GROUP_DIM), POOL_GROUP_DIM ** -0.5),
        'pool_scale': uni((O, D), 0.5, 1.0),
        'ffn_norm': 1.0 + nrm((L, D), 0.02),
        'w_ffn_up': nrm((L, D, 2 * D_FF), D ** -0.5),
        'conv_w': nrm((L, CONV_WIDTH, 2 * D_FF), CONV_WIDTH ** -0.5),
        'conv_b': nrm((L, 2 * D_FF), 0.02),
        'w_ffn_down': nrm((L, D_FF, D), D_FF ** -0.5),
        'final_norm': 1.0 + nrm((D,), 0.02),
    }


def reference(x, mix_norm, w_in, b_f, shift_mu, w0, w_decay_up, a0, w_iclr_up, w_gate_up,
              k_k, k_a, r_k, gn_w, gn_b, w_out, pool_norm, w_pool, pool_scale,
              ffn_norm, w_ffn_up, conv_w, conv_b, w_ffn_down, final_norm):
    e = 0
    o = 0
    for layer in range(DEPTH):
        if layer % 2 == 0:
            x = fox_rwkv_layer(x, mix_norm[e], w_in[e], b_f[e], shift_mu[e], w0[e],
                               w_decay_up[e], a0[e], w_iclr_up[e], w_gate_up[e],
                               k_k[e], k_a[e], r_k[e], gn_w[e], gn_b[e], w_out[e])
            e += 1
        else:
            x = pool_layer(x, pool_norm[o], w_pool[o], pool_scale[o])
            o += 1
        x = conv_ffn(x, ffn_norm[layer], w_ffn_up[layer], conv_w[layer], conv_b[layer],
                     w_ffn_down[layer])
    return rmsnorm(x, final_norm)
```

```python
import functools

import jax
import jax.numpy as jnp
from jax import lax
from jax.experimental import pallas as pl
from jax.experimental.pallas import tpu as pltpu

F32 = jnp.float32
BF16 = jnp.bfloat16

RMS_EPS = 1e-6
GN_EPS = 64e-5
FOX_HEADS = 8
FOX_HEAD_DIM = 128
RWKV_HEAD_DIM = 64
POOL_WINDOWS = (2, 4, 8, 16)
CONV_WIDTH = 3
DECAY_RANK = 96
ICLR_RANK = 96
GATE_RANK = 256

LANES = 128
SUBLANES = 8
VMEM_LIMIT = 56 * 1024 * 1024

RWKV_CHUNK = 64
POOL_HALO = 16
CONV_HALO = 8
NEG_BIG = -1e30


def _cparams(sem):
    return pltpu.CompilerParams(dimension_semantics=sem, vmem_limit_bytes=VMEM_LIMIT)


def _rms(x, g):
    ms = jnp.mean(x * x, axis=-1, keepdims=True)
    return x * lax.rsqrt(ms + RMS_EPS) * g


def _dot(a, b):
    return jnp.dot(a, b, preferred_element_type=F32)


def _dot_nt(a, b):
    return lax.dot_general(a, b, (((1,), (1,)), ((), ())), preferred_element_type=F32)


def _split3(x):
    hi = x.astype(BF16)
    r1 = x - hi.astype(F32)
    mid = r1.astype(BF16)
    lo = (r1 - mid.astype(F32)).astype(BF16)
    return hi, mid, lo


def _cumsum_rows(tri, x):
    hi, mid, lo = _split3(x)
    return _dot(tri, hi) + _dot(tri, mid) + _dot(tri, lo)


def _tril_ones(n, dtype):
    r = lax.broadcasted_iota(jnp.int32, (n, n), 0)
    c = lax.broadcasted_iota(jnp.int32, (n, n), 1)
    return (c <= r).astype(dtype)


def _softplus(x):
    return jnp.maximum(x, 0.0) + jnp.log1p(jnp.exp(-jnp.abs(x)))


def _rms_kernel(x_ref, g_ref, o_ref):
    o_ref[...] = _rms(x_ref[...], g_ref[...]).astype(o_ref.dtype)


def rmsnorm_bf16(x, g, tm=512):
    T, D = x.shape
    return pl.pallas_call(
        _rms_kernel,
        out_shape=jax.ShapeDtypeStruct((T, D), BF16),
        grid=(T // tm,),
        in_specs=[pl.BlockSpec((tm, D), lambda i: (i, 0)),
                  pl.BlockSpec((1, D), lambda i: (0, 0))],
        out_specs=pl.BlockSpec((tm, D), lambda i: (i, 0)),
        compiler_params=_cparams(("parallel",)),
        name="rmsnorm",
    )(x, g.reshape(1, D))


def _mm_kernel(a_ref, w_ref, o_ref):
    o_ref[...] = _dot(a_ref[...], w_ref[...]).astype(o_ref.dtype)


def matmul(a, w, out_dtype, tm=1024, tn=512, name="matmul"):
    M, K = a.shape
    N = w.shape[1]
    tm = min(tm, M)
    tn = min(tn, N)
    return pl.pallas_call(
        _mm_kernel,
        out_shape=jax.ShapeDtypeStruct((M, N), out_dtype),
        grid=(M // tm, pl.cdiv(N, tn)),
        in_specs=[pl.BlockSpec((tm, K), lambda i, j: (i, 0)),
                  pl.BlockSpec((K, tn), lambda i, j: (0, j))],
        out_specs=pl.BlockSpec((tm, tn), lambda i, j: (i, j)),
        compiler_params=_cparams(("parallel", "parallel")),
        name=name,
    )(a, w)


def _outproj_kernel(oa_ref, ob_ref, x_ref, wa_ref, wb_ref, g_ref, xo_ref, ho_ref):
    y = x_ref[...] + _dot(oa_ref[...], wa_ref[...]) + _dot(ob_ref[...], wb_ref[...])
    xo_ref[...] = y
    ho_ref[...] = _rms(y, g_ref[...]).astype(ho_ref.dtype)


def out_projection(o_a, o_b, x, w_a, w_b, g_next, tm=512):
    T, D = x.shape
    Ka, Kb = o_a.shape[1], o_b.shape[1]
    return pl.pallas_call(
        _outproj_kernel,
        out_shape=(jax.ShapeDtypeStruct((T, D), F32), jax.ShapeDtypeStruct((T, D), BF16)),
        grid=(T // tm,),
        in_specs=[pl.BlockSpec((tm, Ka), lambda i: (i, 0)),
                  pl.BlockSpec((tm, Kb), lambda i: (i, 0)),
                  pl.BlockSpec((tm, D), lambda i: (i, 0)),
                  pl.BlockSpec((Ka, D), lambda i: (0, 0)),
                  pl.BlockSpec((Kb, D), lambda i: (0, 0)),
                  pl.BlockSpec((1, D), lambda i: (0, 0))],
        out_specs=(pl.BlockSpec((tm, D), lambda i: (i, 0)),
                   pl.BlockSpec((tm, D), lambda i: (i, 0))),
        compiler_params=_cparams(("parallel",)),
        name="out_projection",
    )(o_a, o_b, x, w_a, w_b, g_next.reshape(1, D))


def _ffn_kernel(h_ref, x_ref, wg_ref, wv_ref, cwg_ref, cwv_ref, wd_ref, gn_ref,
                *rest, tiles_per_seq, emit):
    if emit == "x+h":
        xo_ref, ho_ref, acc_ref, halo_ref = rest
    else:
        xo_ref, acc_ref, halo_ref = rest
        ho_ref = None
    i = pl.program_id(0)
    j = pl.program_id(1)
    nj = pl.num_programs(1)
    tm = h_ref.shape[0]

    @pl.when(j == 0)
    def _():
        acc_ref[...] = x_ref[...]

    @pl.when((i % tiles_per_seq) == 0)
    def _():
        halo_ref[j] = jnp.zeros(halo_ref.shape[1:], F32)

    h = h_ref[...]

    def conv_half(w_ref, cw_ref, slot):
        u = _dot(h, w_ref[...])
        prev = halo_ref[j, slot]
        halo_ref[j, slot] = u[tm - CONV_HALO:, :]
        ext = jnp.concatenate([prev, u], axis=0)
        cw = cw_ref[...]
        u1 = ext[CONV_HALO - 1:CONV_HALO - 1 + tm, :]
        u2 = ext[CONV_HALO - 2:CONV_HALO - 2 + tm, :]
        return u2 * cw[0:1, :] + u1 * cw[1:2, :] + u * cw[2:3, :] + cw[3:4, :]

    gate = conv_half(wg_ref, cwg_ref, 0)
    val = conv_half(wv_ref, cwv_ref, 1)
    act = (gate * jax.nn.sigmoid(gate) * val).astype(BF16)
    acc_ref[...] += _dot(act, wd_ref[...])

    @pl.when(j == nj - 1)
    def _():
        y = acc_ref[...]
        if emit == "final":
            xo_ref[...] = _rms(y, gn_ref[...])
        else:
            xo_ref[...] = y
            if ho_ref is not None:
                ho_ref[...] = _rms(y, gn_ref[...]).astype(ho_ref.dtype)


def conv_ffn(h, x, w_gate, w_val, cw_gate, cw_val, w_down, g_next, *, seq_len, emit,
             tm=512, tf=512):
    T, D = x.shape
    F = w_gate.shape[1]
    nj = F // tf
    kern = functools.partial(_ffn_kernel, tiles_per_seq=seq_len // tm, emit=emit)
    out_shape = [jax.ShapeDtypeStruct((T, D), F32)]
    out_specs = [pl.BlockSpec((tm, D), lambda i, j: (i, 0))]
    if emit == "x+h":
        out_shape.append(jax.ShapeDtypeStruct((T, D), BF16))
        out_specs.append(pl.BlockSpec((tm, D), lambda i, j: (i, 0)))
    res = pl.pallas_call(
        kern,
        out_shape=tuple(out_shape),
        grid=(T // tm, nj),
        in_specs=[pl.BlockSpec((tm, D), lambda i, j: (i, 0)),
                  pl.BlockSpec((tm, D), lambda i, j: (i, 0)),
                  pl.BlockSpec((D, tf), lambda i, j: (0, j)),
                  pl.BlockSpec((D, tf), lambda i, j: (0, j)),
                  pl.BlockSpec((4, tf), lambda i, j: (0, j)),
                  pl.BlockSpec((4, tf), lambda i, j: (0, j)),
                  pl.BlockSpec((tf, D), lambda i, j: (j, 0)),
                  pl.BlockSpec((1, D), lambda i, j: (0, 0))],
        out_specs=tuple(out_specs),
        scratch_shapes=[pltpu.VMEM((tm, D), F32),
                        pltpu.VMEM((nj, 2, CONV_HALO, tf), F32)],
        compiler_params=_cparams(("arbitrary", "arbitrary")),
        name="conv_ffn",
    )(h, x, w_gate, w_val, cw_gate, cw_val, w_down, g_next.reshape(1, D))
    return res


def _pool_kernel(x_ref, gp_ref, w_ref, sc_ref, gn_ref, xo_ref, ho_ref, halo_ref,
                 *, tiles_per_seq):
    i = pl.program_id(0)
    tm, D = x_ref.shape
    G = len(POOL_WINDOWS)
    gd = D // G
    ti = i % tiles_per_seq
    x = x_ref[...]
    h = _rms(x, gp_ref[...])

    @pl.when(ti == 0)
    def _():
        halo_ref[...] = jnp.zeros_like(halo_ref)

    prev = halo_ref[...]
    halo_ref[...] = h[tm - POOL_HALO:, :]
    pos = ti * tm + lax.broadcasted_iota(jnp.int32, (tm, 1), 0)
    outs = []
    for g, win in enumerate(POOL_WINDOWS):
        hg = h[:, g * gd:(g + 1) * gd]
        s = jnp.concatenate([prev[:, g * gd:(g + 1) * gd], hg], axis=0)
        sh = 1
        while sh < win:
            s = s + pltpu.roll(s, sh, 0)
            sh *= 2
        wsum = s[POOL_HALO:, :]
        count = jnp.minimum(pos + 1, win).astype(F32)
        pooled = wsum / count - hg
        y = _dot(pooled.astype(BF16), w_ref[g])
        outs.append(x[:, g * gd:(g + 1) * gd] + y * sc_ref[:, g * gd:(g + 1) * gd])
    xn = jnp.concatenate(outs, axis=1)
    xo_ref[...] = xn
    ho_ref[...] = _rms(xn, gn_ref[...]).astype(ho_ref.dtype)


def pool_mixer(x, g_pool, w_pool, pool_scale, g_next, *, seq_len, tm=512):
    T, D = x.shape
    G, gd, _ = w_pool.shape
    kern = functools.partial(_pool_kernel, tiles_per_seq=seq_len // tm)
    return pl.pallas_call(
        kern,
        out_shape=(jax.ShapeDtypeStruct((T, D), F32), jax.ShapeDtypeStruct((T, D), BF16)),
        grid=(T // tm,),
        in_specs=[pl.BlockSpec((tm, D), lambda i: (i, 0)),
                  pl.BlockSpec((1, D), lambda i: (0, 0)),
                  pl.BlockSpec((G, gd, gd), lambda i: (0, 0, 0)),
                  pl.BlockSpec((1, D), lambda i: (0, 0)),
                  pl.BlockSpec((1, D), lambda i: (0, 0))],
        out_specs=(pl.BlockSpec((tm, D), lambda i: (i, 0)),
                   pl.BlockSpec((tm, D), lambda i: (i, 0))),
        scratch_shapes=[pltpu.VMEM((POOL_HALO, D), F32)],
        compiler_params=_cparams(("arbitrary",)),
        name="pool_mixer",
    )(x, g_pool.reshape(1, D), w_pool, pool_scale.reshape(1, D), g_next.reshape(1, D))


def _fox_cumsum_kernel(f_ref, b_ref, ct_ref, carry_ref, *, tiles_per_seq):
    i = pl.program_id(0)
    tc = f_ref.shape[0]

    @pl.when((i % tiles_per_seq) == 0)
    def _():
        carry_ref[...] = jnp.zeros_like(carry_ref)

    z = f_ref[...] + b_ref[...]
    lf = jnp.minimum(z, 0.0) - jnp.log1p(jnp.exp(-jnp.abs(z)))
    c = _cumsum_rows(_tril_ones(tc, BF16), lf) + carry_ref[...]
    carry_ref[...] = c[tc - 1:tc, :]
    ct_ref[...] = c.T[:ct_ref.shape[0], :]


def fox_cumsum(misc, f_col_block, b_f_padded, n_heads, *, seq_len, tc=256):
    T = misc.shape[0]
    kern = functools.partial(_fox_cumsum_kernel, tiles_per_seq=seq_len // tc)
    return pl.pallas_call(
        kern,
        out_shape=jax.ShapeDtypeStruct((n_heads, T), F32),
        grid=(T // tc,),
        in_specs=[pl.BlockSpec((tc, LANES), lambda i: (i, f_col_block)),
                  pl.BlockSpec((1, LANES), lambda i: (0, 0))],
        out_specs=pl.BlockSpec((n_heads, tc), lambda i: (0, i)),
        scratch_shapes=[pltpu.VMEM((1, LANES), F32)],
        compiler_params=_cparams(("arbitrary",)),
        name="fox_cumsum",
    )(misc, b_f_padded)


def _fox_kernel(q_ref, k_ref, v_ref, ck_ref, o_ref, m_ref, l_ref, acc_ref, *, tk):
    i = pl.program_id(2)
    tq = q_ref.shape[0]
    q = q_ref[...]
    m_ref[...] = jnp.full_like(m_ref, NEG_BIG)
    l_ref[...] = jnp.zeros_like(l_ref)
    acc_ref[...] = jnp.zeros_like(acc_ref)

    def step(j, masked):
        off = pl.multiple_of(j * tk, tk)
        kj = k_ref[pl.ds(off, tk), :]
        vj = v_ref[pl.ds(off, tk), :]
        s = _dot_nt(q, kj) - ck_ref[0, :, pl.ds(off, tk)]
        if masked:
            r = lax.broadcasted_iota(jnp.int32, (tq, tk), 0)
            c = lax.broadcasted_iota(jnp.int32, (tq, tk), 1)
            s = jnp.where(c <= r, s, NEG_BIG)
        m_old = m_ref[...]
        m_new = jnp.maximum(m_old, jnp.max(s, axis=-1, keepdims=True))
        alpha = jnp.exp(m_old - m_new)
        p = jnp.exp(s - m_new)
        l_ref[...] = alpha * l_ref[...] + jnp.sum(p, axis=-1, keepdims=True)
        acc_ref[...] = alpha * acc_ref[...] + _dot(p.astype(BF16), vj)
        m_ref[...] = m_new

    def body(j, carry):
        step(j, False)
        return carry

    lax.fori_loop(0, i, body, 0)
    step(i, True)
    o_ref[...] = (acc_ref[...] / l_ref[...]).astype(o_ref.dtype)


def fox_attention(qkv, ck, *, batch, seq_len, n_heads, tq=512):
    T = qkv.shape[0]
    nq = seq_len // tq
    dh = FOX_HEAD_DIM
    kern = functools.partial(_fox_kernel, tk=tq)
    return pl.pallas_call(
        kern,
        out_shape=jax.ShapeDtypeStruct((T, n_heads * dh), BF16),
        grid=(batch, n_heads, nq),
        in_specs=[pl.BlockSpec((tq, dh), lambda b, h, i: (b * nq + i, h)),
                  pl.BlockSpec((seq_len, dh), lambda b, h, i: (b, n_heads + h)),
                  pl.BlockSpec((seq_len, dh), lambda b, h, i: (b, 2 * n_heads + h)),
                  pl.BlockSpec((1, 1, seq_len), lambda b, h, i: (b * n_heads + h, 0, 0))],
        out_specs=pl.BlockSpec((tq, dh), lambda b, h, i: (b * nq + i, h)),
        scratch_shapes=[pltpu.VMEM((tq, 1), F32), pltpu.VMEM((tq, 1), F32),
                        pltpu.VMEM((tq, dh), F32)],
        compiler_params=_cparams(("parallel", "parallel", "arbitrary")),
        name="fox_attention",
    )(qkv, qkv, qkv, ck)


def _rwkv_kernel(rkv_ref, rkvp_ref, mi_ref, mip_ref, mu_rkv_ref, mu_mi_ref,
                 w0_ref, wdu_ref, a0_ref, wiu_ref, wgu_ref, kk_ref, ka_ref, rk_ref,
                 gnw_ref, gnb_ref, o_ref, state_ref, y_ref):
    c = pl.program_id(1)
    C = rkv_ref.shape[0]
    W = w0_ref.shape[1]
    N = RWKV_HEAD_DIM
    H = W // N

    @pl.when(c == 0)
    def _():
        state_ref[...] = jnp.zeros_like(state_ref)

    row0 = lax.broadcasted_iota(jnp.int32, (C, 1), 0) == 0

    def token_shift_lerp(cur_ref, prev_ref, mu_ref):
        cur = cur_ref[...]
        last = jnp.where(c == 0, 0.0, prev_ref[SUBLANES - 1:SUBLANES, :])
        shifted = jnp.where(row0, last, pltpu.roll(cur, 1, 0))
        return cur + (shifted - cur) * mu_ref[...]

    rw = token_shift_lerp(rkv_ref, rkvp_ref, mu_rkv_ref)
    mi = token_shift_lerp(mi_ref, mip_ref, mu_mi_ref)
    r = rw[:, :W]
    kb = rw[:, W:2 * W]
    vb = rw[:, 2 * W:]
    wd = mi[:, :LANES]
    ad = mi[:, LANES:2 * LANES]
    gd = mi[:, 2 * LANES:]

    lw = -jnp.exp(-_softplus(-(w0_ref[...] + _dot(jnp.tanh(wd).astype(BF16), wdu_ref[...]))) - 0.5)
    a = jax.nn.sigmoid(a0_ref[...] + _dot(ad.astype(BF16), wiu_ref[...]))
    gate = _dot(jax.nn.sigmoid(gd).astype(BF16), wgu_ref[...])
    kkraw = kb * kk_ref[...]
    kmod = kb * (1.0 + (a - 1.0) * ka_ref[...])
    rkr = r * kmod * rk_ref[...]

    G = _cumsum_rows(_tril_ones(C, BF16), lw)
    gam = jnp.exp(G)
    gam_inv = jnp.exp(-G)
    gam_prev = jnp.exp(G - lw)
    GC = G[C - 1:C, :]
    gam_c = gam[C - 1:C, :]
    gam_tail = jnp.exp(GC - G)

    r2 = lax.broadcasted_iota(jnp.int32, (2 * C, 2 * C), 0)
    c2 = lax.broadcasted_iota(jnp.int32, (2 * C, 2 * C), 1)
    rr = jnp.where(r2 >= C, r2 - C, r2)
    cc = jnp.where(c2 >= C, c2 - C, c2)
    amask = cc + jnp.where(r2 >= C, 1, 0) <= rr
    ri = lax.broadcasted_iota(jnp.int32, (C, C), 0)
    ci = lax.broadcasted_iota(jnp.int32, (C, C), 1)
    eye = (ri == ci).astype(F32)

    for hd in range(H):
        sl = slice(hd * N, (hd + 1) * N)
        r_h, v_h, km_h = r[:, sl], vb[:, sl], kmod[:, sl]
        kkr = kkraw[:, sl]
        nrm = jnp.sqrt(jnp.sum(kkr * kkr, axis=-1, keepdims=True))
        kk_h = kkr / jnp.maximum(nrm, 1e-12)
        b_h = a[:, sl] * kk_h
        Rt = r_h * gam[:, sl]
        Kt = km_h * gam_inv[:, sl]
        Bt = b_h * gam_inv[:, sl]
        kt = kk_h * gam_prev[:, sl]
        Kg = km_h * gam_tail[:, sl]
        Bg = b_h * gam_tail[:, sl]

        S0 = state_ref[hd]
        L1 = jnp.concatenate([Rt, kt], axis=0).astype(BF16)
        R1 = jnp.concatenate([Kt, Bt], axis=0).astype(BF16)
        A = jnp.where(amask, _dot_nt(L1, R1), 0.0)
        RH = _dot_nt(L1, S0.astype(BF16))

        Nm = A[C:, C:]
        X = eye - Nm
        P = _dot(Nm.astype(BF16), Nm.astype(BF16))
        span = 2
        while span < C:
            X = X + _dot(X.astype(BF16), P.astype(BF16))
            span *= 2
            if span < C:
                P = _dot(P.astype(BF16), P.astype(BF16))

        vbf = v_h.astype(BF16)
        Z = RH[C:, :] + _dot(A[C:, :C].astype(BF16), vbf)
        U = _dot(X.astype(BF16), Z.astype(BF16))
        VU = jnp.concatenate([v_h, -U], axis=0).astype(BF16)
        Y = RH[:C, :] + _dot(A[:C, :].astype(BF16), VU)
        KB = jnp.concatenate([Kg, Bg], axis=0).astype(BF16)
        state_ref[hd] = S0 * gam_c[:, sl] + lax.dot_general(
            VU, KB, (((0,), (0,)), ((), ())), preferred_element_type=F32)

        mu = jnp.mean(Y, axis=-1, keepdims=True)
        d = Y - mu
        var = jnp.mean(d * d, axis=-1, keepdims=True)
        yn = d * lax.rsqrt(var + GN_EPS) * gnw_ref[:, sl] + gnb_ref[:, sl]
        yn = yn + jnp.sum(rkr[:, sl], axis=-1, keepdims=True) * v_h
        y_ref[:, sl] = yn

    o_ref[...] = (y_ref[...] * gate).astype(o_ref.dtype)


def rwkv7_mixer(rkv, misc, mu_rkv, mu_mi, w0, wdu, a0, wiu, wgu, k_k, k_a, r_k, gn_w, gn_b,
                *, batch, seq_len):
    T, W3 = rkv.shape
    W = W3 // 3
    C = RWKV_CHUNK
    nc = seq_len // C
    MI = 4 * LANES
    H = W // RWKV_HEAD_DIM
    row = lambda v: v.reshape(1, -1)
    full = lambda shape: pl.BlockSpec(shape, lambda b, c: (0,) * len(shape))
    prev_blk = lambda b, c: (jnp.maximum((b * nc + c) * (C // SUBLANES) - 1, 0), 0)
    return pl.pallas_call(
        _rwkv_kernel,
        out_shape=jax.ShapeDtypeStruct((T, W), BF16),
        grid=(batch, nc),
        in_specs=[pl.BlockSpec((C, W3), lambda b, c: (b * nc + c, 0)),
                  pl.BlockSpec((SUBLANES, W3), prev_blk),
                  pl.BlockSpec((C, MI), lambda b, c: (b * nc + c, 0)),
                  pl.BlockSpec((SUBLANES, MI), prev_blk),
                  full((1, W3)), full((1, MI)),
                  full((1, W)), full((LANES, W)), full((1, W)), full((LANES, W)),
                  full((2 * LANES, W)), full((1, W)), full((1, W)), full((1, W)),
                  full((1, W)), full((1, W))],
        out_specs=pl.BlockSpec((C, W), lambda b, c: (b * nc + c, 0)),
        scratch_shapes=[pltpu.VMEM((H, RWKV_HEAD_DIM, RWKV_HEAD_DIM), F32),
                        pltpu.VMEM((C, W), F32)],
        compiler_params=_cparams(("parallel", "arbitrary")),
        name="rwkv7_mixer",
    )(rkv, rkv, misc, misc, row(mu_rkv), row(mu_mi), row(w0), wdu, row(a0), wiu, wgu,
      row(k_k), row(k_a), row(r_k), row(gn_w), row(gn_b))


def _pad_cols(w, n):
    return jnp.pad(w, ((0, 0), (0, n - w.shape[1])))


def _pad_rows(w, n):
    return jnp.pad(w, ((0, n - w.shape[0]), (0, 0)))


def _fox_rwkv_layer(x, h, p, g_next, *, batch, seq_len):
    D = x.shape[1]
    n_fox = FOX_HEADS
    fox_w = n_fox * FOX_HEAD_DIM
    rw_w = D - fox_w
    w_in = p["w_in"]
    fox_cols = 3 * fox_w + n_fox
    scale = FOX_HEAD_DIM ** -0.5

    w_q = w_in[:, :fox_w] * scale
    w_qkv = jnp.concatenate([w_q, w_in[:, fox_w:3 * fox_w]], axis=1).astype(BF16)
    w_f = w_in[:, 3 * fox_w:fox_cols]
    w_rkv = w_in[:, fox_cols:fox_cols + 3 * rw_w].astype(BF16)
    o = fox_cols + 3 * rw_w
    w_wd = w_in[:, o:o + DECAY_RANK]
    w_ad = w_in[:, o + DECAY_RANK:o + DECAY_RANK + ICLR_RANK]
    w_gd = w_in[:, o + DECAY_RANK + ICLR_RANK:]
    w_misc = jnp.concatenate([_pad_cols(w_wd, LANES), _pad_cols(w_ad, LANES), w_gd,
                              _pad_cols(w_f, LANES)], axis=1).astype(BF16)
    mu = p["shift_mu"]
    mu_rkv = mu[:3 * rw_w]
    mu_wd = mu[3 * rw_w:3 * rw_w + DECAY_RANK]
    mu_ad = mu[3 * rw_w + DECAY_RANK:3 * rw_w + DECAY_RANK + ICLR_RANK]
    mu_gd = mu[3 * rw_w + DECAY_RANK + ICLR_RANK:]
    pad1 = lambda v: jnp.pad(v, (0, LANES - v.shape[0]))
    mu_mi = jnp.concatenate([pad1(mu_wd), pad1(mu_ad), mu_gd])

    qkv = matmul(h, w_qkv, BF16, name="proj_qkv")
    rkv = matmul(h, w_rkv, F32, name="proj_rkv")
    misc = matmul(h, w_misc, F32, tn=5 * LANES, name="proj_misc")

    b_f = jnp.pad(p["b_f"], (0, LANES - n_fox)).reshape(1, LANES)
    ct = fox_cumsum(misc, 4, b_f, n_fox, seq_len=seq_len)
    ck = ct.reshape(n_fox, batch, seq_len).transpose(1, 0, 2).reshape(batch * n_fox, 1, seq_len)
    o_a = fox_attention(qkv, ck, batch=batch, seq_len=seq_len, n_heads=n_fox)

    o_b = rwkv7_mixer(
        rkv, misc, mu_rkv, mu_mi, p["w0"],
        _pad_rows(p["w_decay_up"], LANES).astype(BF16), p["a0"],
        _pad_rows(p["w_iclr_up"], LANES).astype(BF16), p["w_gate_up"].astype(BF16),
        p["k_k"], p["k_a"], p["r_k"].reshape(-1), p["gn_w"], p["gn_b"],
        batch=batch, seq_len=seq_len)

    w_out = p["w_out"].astype(BF16)
    return out_projection(o_a, o_b, x, w_out[:fox_w], w_out[fox_w:], g_next)


def _ffn_layer(x, h, w_up, conv_w, conv_b, w_down, g_next, *, seq_len, emit):
    F = w_down.shape[0]
    w_up = w_up.astype(BF16)
    cw = jnp.concatenate([conv_w, conv_b[None, :]], axis=0)
    return conv_ffn(h, x, w_up[:, :F], w_up[:, F:], cw[:, :F], cw[:, F:],
                    w_down.astype(BF16), g_next, seq_len=seq_len, emit=emit)


def kernel(x, mix_norm, w_in, b_f, shift_mu, w0, w_decay_up, a0, w_iclr_up, w_gate_up, k_k, k_a, r_k, gn_w, gn_b, w_out, pool_norm, w_pool, pool_scale, ffn_norm, w_ffn_up, conv_w, conv_b, w_ffn_down, final_norm):
    B, S, D = x.shape
    depth = ffn_norm.shape[0]
    xt = x.reshape(B * S, D)
    h = rmsnorm_bf16(xt, mix_norm[0])
    e = 0
    o = 0
    for layer in range(depth):
        if layer % 2 == 0:
            p = dict(w_in=w_in[e], b_f=b_f[e], shift_mu=shift_mu[e], w0=w0[e],
                     w_decay_up=w_decay_up[e], a0=a0[e], w_iclr_up=w_iclr_up[e],
                     w_gate_up=w_gate_up[e], k_k=k_k[e], k_a=k_a[e], r_k=r_k[e],
                     gn_w=gn_w[e], gn_b=gn_b[e], w_out=w_out[e])
            xt, h = _fox_rwkv_layer(xt, h, p, ffn_norm[layer], batch=B, seq_len=S)
            e += 1
        else:
            xt, h = pool_mixer(xt, pool_norm[o], w_pool[o].astype(BF16), pool_scale[o],
                               ffn_norm[layer], seq_len=S)
            o += 1
        last = layer == depth - 1
        if last:
            g_next, emit = final_norm, "final"
        elif (layer + 1) % 2 == 0:
            g_next, emit = mix_norm[e], "x+h"
        else:
            g_next, emit = pool_norm[o], "x"
        res = _ffn_layer(xt, h, w_ffn_up[layer], conv_w[layer], conv_b[layer],
                         w_ffn_down[layer], g_next, seq_len=S, emit=emit)
        if emit == "x+h":
            xt, h = res
        else:
            xt, h = res[0], None
    return xt.reshape(B, S, D)
```

```python
import functools

import jax
import jax.numpy as jnp
from jax import lax
from jax.experimental import pallas as pl
from jax.experimental.pallas import tpu as pltpu

F32 = jnp.float32
BF16 = jnp.bfloat16

RMS_EPS = 1e-6
GN_EPS = 64e-5
FOX_HEADS = 8
FOX_HEAD_DIM = 128
RWKV_HEAD_DIM = 64
POOL_WINDOWS = (2, 4, 8, 16)
CONV_WIDTH = 3
DECAY_RANK = 96
ICLR_RANK = 96
GATE_RANK = 256

LANES = 128
SUBLANES = 8
VMEM_LIMIT = 56 * 1024 * 1024

RWKV_CHUNK = 64
POOL_HALO = 16
CONV_HALO = 8
NEG_BIG = -1e30
LOG2E = 1.4426950408889634


def _cparams(sem):
    return pltpu.CompilerParams(dimension_semantics=sem, vmem_limit_bytes=VMEM_LIMIT)


def _rms(x, g):
    ms = jnp.mean(x * x, axis=-1, keepdims=True)
    return x * lax.rsqrt(ms + RMS_EPS) * g


def _dot(a, b):
    return jnp.dot(a, b, preferred_element_type=F32)


def _dot_nt(a, b):
    return lax.dot_general(a, b, (((1,), (1,)), ((), ())), preferred_element_type=F32)


def _split3(x):
    hi = x.astype(BF16)
    r1 = x - hi.astype(F32)
    mid = r1.astype(BF16)
    lo = (r1 - mid.astype(F32)).astype(BF16)
    return hi, mid, lo


def _cumsum_rows(tri, x):
    hi, mid, lo = _split3(x)
    return _dot(tri, hi) + _dot(tri, mid) + _dot(tri, lo)


def _tril_ones(n, dtype):
    r = lax.broadcasted_iota(jnp.int32, (n, n), 0)
    c = lax.broadcasted_iota(jnp.int32, (n, n), 1)
    return (c <= r).astype(dtype)


def _softplus(x):
    return jnp.maximum(x, 0.0) + jnp.log1p(jnp.exp(-jnp.abs(x)))


def _rms_kernel(x_ref, g_ref, o_ref):
    o_ref[...] = _rms(x_ref[...], g_ref[...]).astype(o_ref.dtype)


def rmsnorm_bf16(x, g, tm=512):
    T, D = x.shape
    return pl.pallas_call(
        _rms_kernel,
        out_shape=jax.ShapeDtypeStruct((T, D), BF16),
        grid=(T // tm,),
        in_specs=[pl.BlockSpec((tm, D), lambda i: (i, 0)),
                  pl.BlockSpec((1, D), lambda i: (0, 0))],
        out_specs=pl.BlockSpec((tm, D), lambda i: (i, 0)),
        compiler_params=_cparams(("parallel",)),
        name="rmsnorm",
    )(x, g.reshape(1, D))


def _mm_kernel(a_ref, w_ref, o_ref):
    o_ref[...] = _dot(a_ref[...], w_ref[...]).astype(o_ref.dtype)


def matmul(a, w, out_dtype, tm=1024, tn=512, name="matmul"):
    M, K = a.shape
    N = w.shape[1]
    tm = min(tm, M)
    tn = min(tn, N)
    return pl.pallas_call(
        _mm_kernel,
        out_shape=jax.ShapeDtypeStruct((M, N), out_dtype),
        grid=(M // tm, pl.cdiv(N, tn)),
        in_specs=[pl.BlockSpec((tm, K), lambda i, j: (i, 0)),
                  pl.BlockSpec((K, tn), lambda i, j: (0, j))],
        out_specs=pl.BlockSpec((tm, tn), lambda i, j: (i, j)),
        compiler_params=_cparams(("parallel", "parallel")),
        name=name,
    )(a, w)


def _outproj_kernel(oa_ref, ob_ref, x_ref, wa_ref, wb_ref, g_ref, xo_ref, ho_ref):
    y = x_ref[...] + _dot(oa_ref[...], wa_ref[...]) + _dot(ob_ref[...], wb_ref[...])
    xo_ref[...] = y
    ho_ref[...] = _rms(y, g_ref[...]).astype(ho_ref.dtype)


def out_projection(o_a, o_b, x, w_a, w_b, g_next, tm=512):
    T, D = x.shape
    Ka, Kb = o_a.shape[1], o_b.shape[1]
    return pl.pallas_call(
        _outproj_kernel,
        out_shape=(jax.ShapeDtypeStruct((T, D), F32), jax.ShapeDtypeStruct((T, D), BF16)),
        grid=(T // tm,),
        in_specs=[pl.BlockSpec((tm, Ka), lambda i: (i, 0)),
                  pl.BlockSpec((tm, Kb), lambda i: (i, 0)),
                  pl.BlockSpec((tm, D), lambda i: (i, 0)),
                  pl.BlockSpec((Ka, D), lambda i: (0, 0)),
                  pl.BlockSpec((Kb, D), lambda i: (0, 0)),
                  pl.BlockSpec((1, D), lambda i: (0, 0))],
        out_specs=(pl.BlockSpec((tm, D), lambda i: (i, 0)),
                   pl.BlockSpec((tm, D), lambda i: (i, 0))),
        compiler_params=_cparams(("parallel",)),
        name="out_projection",
    )(o_a, o_b, x, w_a, w_b, g_next.reshape(1, D))


def _ffn_kernel(h_ref, x_ref, wg_ref, wv_ref, cwg_ref, cwv_ref, wd_ref, gn_ref,
                *rest, tiles_per_seq, emit):
    if emit == "x+h":
        xo_ref, ho_ref, acc_ref, halo_ref = rest
    else:
        xo_ref, acc_ref, halo_ref = rest
        ho_ref = None
    i = pl.program_id(0)
    j = pl.program_id(1)
    nj = pl.num_programs(1)
    tm = h_ref.shape[0]

    @pl.when(j == 0)
    def _():
        acc_ref[...] = x_ref[...]

    @pl.when((i % tiles_per_seq) == 0)
    def _():
        halo_ref[j] = jnp.zeros(halo_ref.shape[1:], F32)

    h = h_ref[...]

    def conv_half(w_ref, cw_ref, slot):
        u = _dot(h, w_ref[...])
        prev = halo_ref[j, slot]
        halo_ref[j, slot] = u[tm - CONV_HALO:, :]
        ext = jnp.concatenate([prev, u], axis=0)
        cw = cw_ref[...]
        u1 = ext[CONV_HALO - 1:CONV_HALO - 1 + tm, :]
        u2 = ext[CONV_HALO - 2:CONV_HALO - 2 + tm, :]
        return u2 * cw[0:1, :] + u1 * cw[1:2, :] + u * cw[2:3, :] + cw[3:4, :]

    gate = conv_half(wg_ref, cwg_ref, 0)
    val = conv_half(wv_ref, cwv_ref, 1)
    act = (gate * jax.nn.sigmoid(gate) * val).astype(BF16)
    acc_ref[...] += _dot(act, wd_ref[...])

    @pl.when(j == nj - 1)
    def _():
        y = acc_ref[...]
        if emit == "final":
            xo_ref[...] = _rms(y, gn_ref[...])
        else:
            xo_ref[...] = y
            if ho_ref is not None:
                ho_ref[...] = _rms(y, gn_ref[...]).astype(ho_ref.dtype)


def conv_ffn(h, x, w_up, cw, w_down, g_next, *, seq_len, emit, tm=512, tf=512):
    T, D = x.shape
    F = w_down.shape[0]
    nj = F // tf
    kern = functools.partial(_ffn_kernel, tiles_per_seq=seq_len // tm, emit=emit)
    out_shape = [jax.ShapeDtypeStruct((T, D), F32)]
    out_specs = [pl.BlockSpec((tm, D), lambda i, j: (i, 0))]
    if emit == "x+h":
        out_shape.append(jax.ShapeDtypeStruct((T, D), BF16))
        out_specs.append(pl.BlockSpec((tm, D), lambda i, j: (i, 0)))
    res = pl.pallas_call(
        kern,
        out_shape=tuple(out_shape),
        grid=(T // tm, nj),
        in_specs=[pl.BlockSpec((tm, D), lambda i, j: (i, 0)),
                  pl.BlockSpec((tm, D), lambda i, j: (i, 0)),
                  pl.BlockSpec((D, tf), lambda i, j: (0, j)),
                  pl.BlockSpec((D, tf), lambda i, j: (0, j + nj)),
                  pl.BlockSpec((4, tf), lambda i, j: (0, j)),
                  pl.BlockSpec((4, tf), lambda i, j: (0, j + nj)),
                  pl.BlockSpec((tf, D), lambda i, j: (j, 0)),
                  pl.BlockSpec((1, D), lambda i, j: (0, 0))],
        out_specs=tuple(out_specs),
        scratch_shapes=[pltpu.VMEM((tm, D), F32),
                        pltpu.VMEM((nj, 2, CONV_HALO, tf), F32)],
        compiler_params=_cparams(("arbitrary", "arbitrary")),
        name="conv_ffn",
    )(h, x, w_up, w_up, cw, cw, w_down, g_next.reshape(1, D))
    return res


def _pool_kernel(x_ref, gp_ref, w_ref, sc_ref, gn_ref, xo_ref, ho_ref, halo_ref,
                 *, tiles_per_seq):
    i = pl.program_id(0)
    tm, D = x_ref.shape
    G = len(POOL_WINDOWS)
    gd = D // G
    ti = i % tiles_per_seq
    x = x_ref[...]
    h = _rms(x, gp_ref[...])

    @pl.when(ti == 0)
    def _():
        halo_ref[...] = jnp.zeros_like(halo_ref)

    prev = halo_ref[...]
    halo_ref[...] = h[tm - POOL_HALO:, :]
    pos = ti * tm + lax.broadcasted_iota(jnp.int32, (tm, 1), 0)
    outs = []
    for g, win in enumerate(POOL_WINDOWS):
        hg = h[:, g * gd:(g + 1) * gd]
        s = jnp.concatenate([prev[:, g * gd:(g + 1) * gd], hg], axis=0)
        sh = 1
        while sh < win:
            s = s + pltpu.roll(s, sh, 0)
            sh *= 2
        wsum = s[POOL_HALO:, :]
        count = jnp.minimum(pos + 1, win).astype(F32)
        pooled = wsum / count - hg
        y = _dot(pooled.astype(BF16), w_ref[g])
        outs.append(x[:, g * gd:(g + 1) * gd] + y * sc_ref[:, g * gd:(g + 1) * gd])
    xn = jnp.concatenate(outs, axis=1)
    xo_ref[...] = xn
    ho_ref[...] = _rms(xn, gn_ref[...]).astype(ho_ref.dtype)


def pool_mixer(x, g_pool, w_pool, pool_scale, g_next, *, seq_len, tm=512):
    T, D = x.shape
    G, gd, _ = w_pool.shape
    kern = functools.partial(_pool_kernel, tiles_per_seq=seq_len // tm)
    return pl.pallas_call(
        kern,
        out_shape=(jax.ShapeDtypeStruct((T, D), F32), jax.ShapeDtypeStruct((T, D), BF16)),
        grid=(T // tm,),
        in_specs=[pl.BlockSpec((tm, D), lambda i: (i, 0)),
                  pl.BlockSpec((1, D), lambda i: (0, 0)),
                  pl.BlockSpec((G, gd, gd), lambda i: (0, 0, 0)),
                  pl.BlockSpec((1, D), lambda i: (0, 0)),
                  pl.BlockSpec((1, D), lambda i: (0, 0))],
        out_specs=(pl.BlockSpec((tm, D), lambda i: (i, 0)),
                   pl.BlockSpec((tm, D), lambda i: (i, 0))),
        scratch_shapes=[pltpu.VMEM((POOL_HALO, D), F32)],
        compiler_params=_cparams(("arbitrary",)),
        name="pool_mixer",
    )(x, g_pool.reshape(1, D), w_pool, pool_scale.reshape(1, D), g_next.reshape(1, D))


def _fox_cumsum_kernel(f_ref, b_ref, ct_ref, carry_ref, *, tiles_per_seq):
    i = pl.program_id(0)
    tc = f_ref.shape[0]

    @pl.when((i % tiles_per_seq) == 0)
    def _():
        carry_ref[...] = jnp.zeros_like(carry_ref)

    z = f_ref[...] + b_ref[...]
    lf = jnp.minimum(z, 0.0) - jnp.log1p(jnp.exp(-jnp.abs(z)))
    c = _cumsum_rows(_tril_ones(tc, BF16), lf) + carry_ref[...]
    carry_ref[...] = c[tc - 1:tc, :]
    ct_ref[...] = (c * LOG2E).T[:ct_ref.shape[0], :]


def fox_cumsum(misc, f_col_block, b_f_padded, n_heads, *, seq_len, tc=256):
    T = misc.shape[0]
    kern = functools.partial(_fox_cumsum_kernel, tiles_per_seq=seq_len // tc)
    return pl.pallas_call(
        kern,
        out_shape=jax.ShapeDtypeStruct((n_heads, T), F32),
        grid=(T // tc,),
        in_specs=[pl.BlockSpec((tc, LANES), lambda i: (i, f_col_block)),
                  pl.BlockSpec((1, LANES), lambda i: (0, 0))],
        out_specs=pl.BlockSpec((n_heads, tc), lambda i: (0, i)),
        scratch_shapes=[pltpu.VMEM((1, LANES), F32)],
        compiler_params=_cparams(("arbitrary",)),
        name="fox_cumsum",
    )(misc, b_f_padded)


def _fox_kernel(q_ref, k_ref, v_ref, ck_ref, o_ref, vaug_ref, m_ref, acc_ref, *, tk, nsplit):
    i = pl.program_id(2)
    tq, dh = q_ref.shape
    hq = tq // nsplit
    nslab = tk // LANES

    @pl.when(i == 0)
    def _():
        vaug_ref[:, :dh] = v_ref[...]
        vaug_ref[:, dh:] = jnp.ones((vaug_ref.shape[0], dh), BF16)

    m_ref[...] = jnp.full_like(m_ref, NEG_BIG)
    acc_ref[...] = jnp.zeros_like(acc_ref)

    def step(j, masked):
        off = pl.multiple_of(j * tk, tk)
        kj = k_ref[pl.ds(off, tk), :]
        vj = vaug_ref[pl.ds(off, tk), :]
        ckj = ck_ref[0, :, pl.ds(off, tk)]
        rows = [slice(h * hq, (h + 1) * hq) for h in range(nsplit)]
        ss = []
        for h in range(nsplit):
            s = _dot_nt(q_ref[rows[h], :], kj) - ckj
            if masked:
                r = lax.broadcasted_iota(jnp.int32, (hq, tk), 0) + h * hq
                c = lax.broadcasted_iota(jnp.int32, (hq, tk), 1)
                s = jnp.where(c <= r, s, NEG_BIG)
            ss.append(s)
        ps, alphas = [], []
        for h in range(nsplit):
            slabs = [ss[h][:, k * LANES:(k + 1) * LANES] for k in range(nslab)]
            smax = functools.reduce(jnp.maximum, slabs)
            m_old = m_ref[rows[h], :]
            m_new = jnp.maximum(m_old, jnp.max(smax, axis=-1, keepdims=True))
            m_ref[rows[h], :] = m_new
            alphas.append(jnp.exp2(m_old - m_new))
            ps.append(jnp.concatenate([jnp.exp2(sl - m_new).astype(BF16) for sl in slabs], axis=1))
        for h in range(nsplit):
            alpha2 = jnp.concatenate([alphas[h], alphas[h]], axis=1)
            acc_ref[rows[h], :] = alpha2 * acc_ref[rows[h], :] + _dot(ps[h], vj)

    def body(j, carry):
        step(j, False)
        return carry

    lax.fori_loop(0, i, body, 0)
    step(i, True)
    acc = acc_ref[...]
    o_ref[...] = (acc[:, :dh] / acc[:, dh:]).astype(o_ref.dtype)


def fox_attention(qkv, ck, *, batch, seq_len, n_heads, tq=512, nsplit=4):
    T = qkv.shape[0]
    nq = seq_len // tq
    dh = FOX_HEAD_DIM
    kern = functools.partial(_fox_kernel, tk=tq, nsplit=nsplit)
    return pl.pallas_call(
        kern,
        out_shape=jax.ShapeDtypeStruct((T, n_heads * dh), BF16),
        grid=(batch, n_heads, nq),
        in_specs=[pl.BlockSpec((tq, dh), lambda b, h, i: (b * nq + i, h)),
                  pl.BlockSpec((seq_len, dh), lambda b, h, i: (b, n_heads + h)),
                  pl.BlockSpec((seq_len, dh), lambda b, h, i: (b, 2 * n_heads + h)),
                  pl.BlockSpec((1, 1, seq_len), lambda b, h, i: (b * n_heads + h, 0, 0))],
        out_specs=pl.BlockSpec((tq, dh), lambda b, h, i: (b * nq + i, h)),
        scratch_shapes=[pltpu.VMEM((seq_len, 2 * dh), BF16), pltpu.VMEM((tq, dh), F32),
                        pltpu.VMEM((tq, 2 * dh), F32)],
        compiler_params=_cparams(("parallel", "parallel", "arbitrary")),
        name="fox_attention",
    )(qkv, qkv, qkv, ck)


def _rwkv_kernel(rkv_ref, rkvp_ref, mi_ref, mip_ref, mu_rkv_ref, mu_mi_ref,
                 w0_ref, wdu_ref, a0_ref, wiu_ref, wgu_ref, kk_ref, ka_ref, rk_ref,
                 gnw_ref, gnb_ref, o_ref, state_ref, y_ref):
    c = pl.program_id(0)
    B, C, _ = rkv_ref.shape
    W = w0_ref.shape[1]
    N = RWKV_HEAD_DIM
    H = W // N

    @pl.when(c == 0)
    def _():
        state_ref[...] = jnp.zeros_like(state_ref)

    row0 = lax.broadcasted_iota(jnp.int32, (C, 1), 0) == 0
    tri = _tril_ones(C, BF16)

    def token_shift_lerp(cur, prev8, mu_ref):
        last = jnp.where(c == 0, 0.0, prev8[SUBLANES - 1:SUBLANES, :])
        shifted = jnp.where(row0, last, pltpu.roll(cur, 1, 0))
        return cur + (shifted - cur) * mu_ref[...]

    r2 = lax.broadcasted_iota(jnp.int32, (2 * C, 2 * C), 0)
    c2 = lax.broadcasted_iota(jnp.int32, (2 * C, 2 * C), 1)
    rr = jnp.where(r2 >= C, r2 - C, r2)
    cc = jnp.where(c2 >= C, c2 - C, c2)
    amask = cc + jnp.where(r2 >= C, 1, 0) <= rr
    ri = lax.broadcasted_iota(jnp.int32, (C, C), 0)
    ci = lax.broadcasted_iota(jnp.int32, (C, C), 1)
    eye = (ri == ci).astype(F32)

    probs = []
    gates = []
    for b in range(B):
        rw = token_shift_lerp(rkv_ref[b], rkvp_ref[b], mu_rkv_ref)
        mi = token_shift_lerp(mi_ref[b], mip_ref[b], mu_mi_ref)
        r = rw[:, :W]
        kb = rw[:, W:2 * W]
        vb = rw[:, 2 * W:]
        wd = mi[:, :LANES]
        ad = mi[:, LANES:2 * LANES]
        gd = mi[:, 2 * LANES:]

        lw = -jnp.exp(-_softplus(-(w0_ref[...] + _dot(jnp.tanh(wd).astype(BF16), wdu_ref[...]))) - 0.5)
        a = jax.nn.sigmoid(a0_ref[...] + _dot(ad.astype(BF16), wiu_ref[...]))
        gates.append(_dot(jax.nn.sigmoid(gd).astype(BF16), wgu_ref[...]))
        kkraw = kb * kk_ref[...]
        kmod = kb * (1.0 + (a - 1.0) * ka_ref[...])
        rkr = r * kmod * rk_ref[...]

        G = _cumsum_rows(tri, lw)
        gam = jnp.exp(G)
        gam_inv = jnp.exp(-G)
        gam_prev = jnp.exp(G - lw)
        gam_c = gam[C - 1:C, :]
        gam_tail = jnp.exp(G[C - 1:C, :] - G)

        for hd in range(H):
            sl = slice(hd * N, (hd + 1) * N)
            r_h, v_h, km_h = r[:, sl], vb[:, sl], kmod[:, sl]
            kkr = kkraw[:, sl]
            nrm = jnp.sqrt(jnp.sum(kkr * kkr, axis=-1, keepdims=True))
            kk_h = kkr / jnp.maximum(nrm, 1e-12)
            b_h = a[:, sl] * kk_h
            probs.append(dict(
                b=b, hd=hd, sl=sl, v=v_h,
                L1=jnp.concatenate([r_h * gam[:, sl], kk_h * gam_prev[:, sl]], axis=0).astype(BF16),
                R1=jnp.concatenate([km_h * gam_inv[:, sl], b_h * gam_inv[:, sl]], axis=0).astype(BF16),
                KB=jnp.concatenate([km_h * gam_tail[:, sl], b_h * gam_tail[:, sl]], axis=0).astype(BF16),
                gam_c=gam_c[:, sl],
                bonus=jnp.sum(rkr[:, sl], axis=-1, keepdims=True) * v_h))

    for p in probs:
        p["S0"] = state_ref[p["b"] * H + p["hd"]]
    for p in probs:
        p["A"] = jnp.where(amask, _dot_nt(p["L1"], p["R1"]), 0.0)
    for p in probs:
        p["RH"] = _dot_nt(p["L1"], p["S0"].astype(BF16))
    for p in probs:
        nm = p["A"][C:, C:]
        p["X"] = eye - nm
        nmb = nm.astype(BF16)
        p["P"] = _dot(nmb, nmb)
    span = 2
    while span < C:
        for p in probs:
            p["Pb"] = p["P"].astype(BF16)
            p["X"] = p["X"] + _dot(p["X"].astype(BF16), p["Pb"])
        span *= 2
        if span < C:
            for p in probs:
                p["P"] = _dot(p["Pb"], p["Pb"])
    for p in probs:
        p["Z"] = p["RH"][C:, :] + _dot(p["A"][C:, :C].astype(BF16), p["v"].astype(BF16))
    for p in probs:
        p["U"] = _dot(p["X"].astype(BF16), p["Z"].astype(BF16))
    for p in probs:
        p["VU"] = jnp.concatenate([p["v"], -p["U"]], axis=0).astype(BF16)
        p["Y"] = p["RH"][:C, :] + _dot(p["A"][:C, :].astype(BF16), p["VU"])
    for p in probs:
        state_ref[p["b"] * H + p["hd"]] = p["S0"] * p["gam_c"] + lax.dot_general(
            p["VU"], p["KB"], (((0,), (0,)), ((), ())), preferred_element_type=F32)
    for p in probs:
        Y = p["Y"]
        mu = jnp.mean(Y, axis=-1, keepdims=True)
        d = Y - mu
        var = jnp.mean(d * d, axis=-1, keepdims=True)
        yn = d * lax.rsqrt(var + GN_EPS) * gnw_ref[:, p["sl"]] + gnb_ref[:, p["sl"]]
        y_ref[p["b"], :, p["sl"]] = yn + p["bonus"]

    for b in range(B):
        o_ref[b] = (y_ref[b] * gates[b]).astype(o_ref.dtype)


def rwkv7_mixer(rkv, misc, mu_rkv, mu_mi, w0, wdu, a0, wiu, wgu, k_k, k_a, r_k, gn_w, gn_b,
                *, batch, seq_len):
    T, W3 = rkv.shape
    W = W3 // 3
    C = RWKV_CHUNK
    nc = seq_len // C
    MI = 4 * LANES
    H = W // RWKV_HEAD_DIM
    rkv3 = rkv.reshape(batch, seq_len, W3)
    misc3 = misc.reshape(batch, seq_len, misc.shape[1])
    row = lambda v: v.reshape(1, -1)
    full = lambda shape: pl.BlockSpec(shape, lambda c: (0,) * len(shape))
    cur_blk = lambda c: (0, c, 0)
    prev_blk = lambda c: (0, jnp.maximum(c * (C // SUBLANES) - 1, 0), 0)
    out = pl.pallas_call(
        _rwkv_kernel,
        out_shape=jax.ShapeDtypeStruct((batch, seq_len, W), BF16),
        grid=(nc,),
        in_specs=[pl.BlockSpec((batch, C, W3), cur_blk),
                  pl.BlockSpec((batch, SUBLANES, W3), prev_blk),
                  pl.BlockSpec((batch, C, MI), cur_blk),
                  pl.BlockSpec((batch, SUBLANES, MI), prev_blk),
                  full((1, W3)), full((1, MI)),
                  full((1, W)), full((LANES, W)), full((1, W)), full((LANES, W)),
                  full((2 * LANES, W)), full((1, W)), full((1, W)), full((1, W)),
                  full((1, W)), full((1, W))],
        out_specs=pl.BlockSpec((batch, C, W), cur_blk),
        scratch_shapes=[pltpu.VMEM((batch * H, RWKV_HEAD_DIM, RWKV_HEAD_DIM), F32),
                        pltpu.VMEM((batch, C, W), F32)],
        compiler_params=_cparams(("arbitrary",)),
        name="rwkv7_mixer",
    )(rkv3, rkv3, misc3, misc3, row(mu_rkv), row(mu_mi), row(w0), wdu, row(a0), wiu, wgu,
      row(k_k), row(k_a), row(r_k), row(gn_w), row(gn_b))
    return out.reshape(T, W)


def _pad_cols(w, n):
    return jnp.pad(w, ((0, 0), (0, n - w.shape[1])))


def _pad_rows(w, n):
    return jnp.pad(w, ((0, n - w.shape[0]), (0, 0)))


def _fox_rwkv_layer(x, h, p, g_next, *, batch, seq_len):
    D = x.shape[1]
    n_fox = FOX_HEADS
    fox_w = n_fox * FOX_HEAD_DIM
    rw_w = D - fox_w
    w_in = p["w_in"]
    fox_cols = 3 * fox_w + n_fox
    scale = FOX_HEAD_DIM ** -0.5

    w_q = w_in[:, :fox_w] * (scale * LOG2E)
    w_qkv = jnp.concatenate([w_q, w_in[:, fox_w:3 * fox_w]], axis=1).astype(BF16)
    w_f = w_in[:, 3 * fox_w:fox_cols]
    w_rkv = w_in[:, fox_cols:fox_cols + 3 * rw_w].astype(BF16)
    o = fox_cols + 3 * rw_w
    w_wd = w_in[:, o:o + DECAY_RANK]
    w_ad = w_in[:, o + DECAY_RANK:o + DECAY_RANK + ICLR_RANK]
    w_gd = w_in[:, o + DECAY_RANK + ICLR_RANK:]
    w_misc = jnp.concatenate([_pad_cols(w_wd, LANES), _pad_cols(w_ad, LANES), w_gd,
                              _pad_cols(w_f, LANES)], axis=1).astype(BF16)
    mu = p["shift_mu"]
    mu_rkv = mu[:3 * rw_w]
    mu_wd = mu[3 * rw_w:3 * rw_w + DECAY_RANK]
    mu_ad = mu[3 * rw_w + DECAY_RANK:3 * rw_w + DECAY_RANK + ICLR_RANK]
    mu_gd = mu[3 * rw_w + DECAY_RANK + ICLR_RANK:]
    pad1 = lambda v: jnp.pad(v, (0, LANES - v.shape[0]))
    mu_mi = jnp.concatenate([pad1(mu_wd), pad1(mu_ad), mu_gd])

    qkv = matmul(h, w_qkv, BF16, name="proj_qkv")
    rkv = matmul(h, w_rkv, F32, name="proj_rkv")
    misc = matmul(h, w_misc, F32, tn=5 * LANES, name="proj_misc")

    b_f = jnp.pad(p["b_f"], (0, LANES - n_fox)).reshape(1, LANES)
    ct = fox_cumsum(misc, 4, b_f, n_fox, seq_len=seq_len)
    ck = ct.reshape(n_fox, batch, seq_len).transpose(1, 0, 2).reshape(batch * n_fox, 1, seq_len)
    o_a = fox_attention(qkv, ck, batch=batch, seq_len=seq_len, n_heads=n_fox)

    o_b = rwkv7_mixer(
        rkv, misc, mu_rkv, mu_mi, p["w0"],
        _pad_rows(p["w_decay_up"], LANES).astype(BF16), p["a0"],
        _pad_rows(p["w_iclr_up"], LANES).astype(BF16), p["w_gate_up"].astype(BF16),
        p["k_k"], p["k_a"], p["r_k"].reshape(-1), p["gn_w"], p["gn_b"],
        batch=batch, seq_len=seq_len)

    w_out = p["w_out"].astype(BF16)
    return out_projection(o_a, o_b, x, w_out[:fox_w], w_out[fox_w:], g_next)


def _ffn_layer(x, h, w_up, conv_w, conv_b, w_down, g_next, *, seq_len, emit):
    cw = jnp.concatenate([conv_w, conv_b[None, :]], axis=0)
    return conv_ffn(h, x, w_up.astype(BF16), cw, w_down.astype(BF16), g_next,
                    seq_len=seq_len, emit=emit)


def kernel(x, mix_norm, w_in, b_f, shift_mu, w0, w_decay_up, a0, w_iclr_up, w_gate_up, k_k, k_a, r_k, gn_w, gn_b, w_out, pool_norm, w_pool, pool_scale, ffn_norm, w_ffn_up, conv_w, conv_b, w_ffn_down, final_norm):
    B, S, D = x.shape
    depth = ffn_norm.shape[0]
    xt = x.reshape(B * S, D)
    h = rmsnorm_bf16(xt, mix_norm[0])
    e = 0
    o = 0
    for layer in range(depth):
        if layer % 2 == 0:
            p = dict(w_in=w_in[e], b_f=b_f[e], shift_mu=shift_mu[e], w0=w0[e],
                     w_decay_up=w_decay_up[e], a0=a0[e], w_iclr_up=w_iclr_up[e],
                     w_gate_up=w_gate_up[e], k_k=k_k[e], k_a=k_a[e], r_k=r_k[e],
                     gn_w=gn_w[e], gn_b=gn_b[e], w_out=w_out[e])
            xt, h = _fox_rwkv_layer(xt, h, p, ffn_norm[layer], batch=B, seq_len=S)
            e += 1
        else:
            xt, h = pool_mixer(xt, pool_norm[o], w_pool[o].astype(BF16), pool_scale[o],
                               ffn_norm[layer], seq_len=S)
            o += 1
        last = layer == depth - 1
        if last:
            g_next, emit = final_norm, "final"
        elif (layer + 1) % 2 == 0:
            g_next, emit = mix_norm[e], "x+h"
        else:
            g_next, emit = pool_norm[o], "x"
        res = _ffn_layer(xt, h, w_ffn_up[layer], conv_w[layer], conv_b[layer],
                         w_ffn_down[layer], g_next, seq_len=S, emit=emit)
        if emit == "x+h":
            xt, h = res
        else:
            xt, h = res[0], None
    return xt.reshape(B, S, D)
```

```python
import functools

import jax
import jax.numpy as jnp
from jax import lax
from jax.experimental import pallas as pl
from jax.experimental.pallas import tpu as pltpu

F32 = jnp.float32
BF16 = jnp.bfloat16

RMS_EPS = 1e-6
GN_EPS = 64e-5
FOX_HEADS = 8
FOX_HEAD_DIM = 128
RWKV_HEAD_DIM = 64
POOL_WINDOWS = (2, 4, 8, 16)
CONV_WIDTH = 3
DECAY_RANK = 96
ICLR_RANK = 96
GATE_RANK = 256

LANES = 128
SUBLANES = 8
VMEM_LIMIT = 56 * 1024 * 1024

RWKV_CHUNK = 64
POOL_HALO = 16
CONV_HALO = 8
NEG_BIG = -1e30
LOG2E = 1.4426950408889634
DECAY_SCALE = 0.6065306597126334


def _cparams(sem):
    return pltpu.CompilerParams(dimension_semantics=sem, vmem_limit_bytes=VMEM_LIMIT)


def _rms(x, g):
    ms = jnp.mean(x * x, axis=-1, keepdims=True)
    return x * lax.rsqrt(ms + RMS_EPS) * g


def _dot(a, b):
    return jnp.dot(a, b, preferred_element_type=F32)


def _dot_nt(a, b):
    return lax.dot_general(a, b, (((1,), (1,)), ((), ())), preferred_element_type=F32)


def _split3(x):
    hi = x.astype(BF16)
    r1 = x - hi.astype(F32)
    mid = r1.astype(BF16)
    lo = (r1 - mid.astype(F32)).astype(BF16)
    return hi, mid, lo


def _cumsum_rows(tri, x):
    hi, mid, lo = _split3(x)
    return _dot(tri, hi) + _dot(tri, mid) + _dot(tri, lo)


def _tril_ones(n, dtype):
    r = lax.broadcasted_iota(jnp.int32, (n, n), 0)
    c = lax.broadcasted_iota(jnp.int32, (n, n), 1)
    return (c <= r).astype(dtype)


def _softplus(x):
    return jnp.maximum(x, 0.0) + jnp.log1p(jnp.exp(-jnp.abs(x)))


def _rms_kernel(x_ref, g_ref, o_ref):
    o_ref[...] = _rms(x_ref[...], g_ref[...]).astype(o_ref.dtype)


def rmsnorm_bf16(x, g, tm=512):
    T, D = x.shape
    return pl.pallas_call(
        _rms_kernel,
        out_shape=jax.ShapeDtypeStruct((T, D), BF16),
        grid=(T // tm,),
        in_specs=[pl.BlockSpec((tm, D), lambda i: (i, 0)),
                  pl.BlockSpec((1, D), lambda i: (0, 0))],
        out_specs=pl.BlockSpec((tm, D), lambda i: (i, 0)),
        compiler_params=_cparams(("parallel",)),
        name="rmsnorm",
    )(x, g.reshape(1, D))


def _mm_kernel(a_ref, w_ref, o_ref):
    o_ref[...] = _dot(a_ref[...], w_ref[...]).astype(o_ref.dtype)


def matmul(a, w, out_dtype, tm=1024, tn=512, name="matmul"):
    M, K = a.shape
    N = w.shape[1]
    tm = min(tm, M)
    tn = min(tn, N)
    return pl.pallas_call(
        _mm_kernel,
        out_shape=jax.ShapeDtypeStruct((M, N), out_dtype),
        grid=(M // tm, pl.cdiv(N, tn)),
        in_specs=[pl.BlockSpec((tm, K), lambda i, j: (i, 0)),
                  pl.BlockSpec((K, tn), lambda i, j: (0, j))],
        out_specs=pl.BlockSpec((tm, tn), lambda i, j: (i, j)),
        compiler_params=_cparams(("parallel", "parallel")),
        name=name,
    )(a, w)


def _outproj_kernel(oa_ref, ob_ref, x_ref, wa_ref, wb_ref, g_ref, xo_ref, ho_ref):
    y = x_ref[...] + _dot(oa_ref[...], wa_ref[...]) + _dot(ob_ref[...], wb_ref[...])
    xo_ref[...] = y
    ho_ref[...] = _rms(y, g_ref[...]).astype(ho_ref.dtype)


def out_projection(o_a, o_b, x, w_a, w_b, g_next, tm=512):
    T, D = x.shape
    Ka, Kb = o_a.shape[1], o_b.shape[1]
    return pl.pallas_call(
        _outproj_kernel,
        out_shape=(jax.ShapeDtypeStruct((T, D), F32), jax.ShapeDtypeStruct((T, D), BF16)),
        grid=(T // tm,),
        in_specs=[pl.BlockSpec((tm, Ka), lambda i: (i, 0)),
                  pl.BlockSpec((tm, Kb), lambda i: (i, 0)),
                  pl.BlockSpec((tm, D), lambda i: (i, 0)),
                  pl.BlockSpec((Ka, D), lambda i: (0, 0)),
                  pl.BlockSpec((Kb, D), lambda i: (0, 0)),
                  pl.BlockSpec((1, D), lambda i: (0, 0))],
        out_specs=(pl.BlockSpec((tm, D), lambda i: (i, 0)),
                   pl.BlockSpec((tm, D), lambda i: (i, 0))),
        compiler_params=_cparams(("parallel",)),
        name="out_projection",
    )(o_a, o_b, x, w_a, w_b, g_next.reshape(1, D))


def _ffn_kernel(h_ref, x_ref, wg_ref, wv_ref, cwg_ref, cwv_ref, wd_ref, gn_ref,
                *rest, tiles_per_seq, emit):
    if emit == "x+h":
        xo_ref, ho_ref, acc_ref, halo_ref, act_ref = rest
    else:
        xo_ref, acc_ref, halo_ref, act_ref = rest
        ho_ref = None
    i = pl.program_id(0)
    j = pl.program_id(1)
    nj = pl.num_programs(1) - 1
    tm = h_ref.shape[0]

    @pl.when(jnp.logical_and((i % tiles_per_seq) == 0, j < nj))
    def _():
        halo_ref[j] = jnp.zeros(halo_ref.shape[1:], F32)

    def up_conv(w_ref, cw_ref, slot):
        u = _dot(h_ref[...], w_ref[...])
        prev = halo_ref[j, slot]
        halo_ref[j, slot] = u[tm - CONV_HALO:, :]
        ext = jnp.concatenate([prev, u], axis=0)
        cw = cw_ref[...]
        u1 = ext[CONV_HALO - 1:CONV_HALO - 1 + tm, :]
        u2 = ext[CONV_HALO - 2:CONV_HALO - 2 + tm, :]
        return u2 * cw[0:1, :] + u1 * cw[1:2, :] + u * cw[2:3, :] + cw[3:4, :]

    def up_tile(down_previous):
        gate = up_conv(wg_ref, cwg_ref, 0)
        val = up_conv(wv_ref, cwv_ref, 1)
        if down_previous:
            acc_ref[...] += _dot(act_ref[...], wd_ref[...])
        act_ref[...] = (gate * jax.nn.sigmoid(gate) * val).astype(BF16)

    @pl.when(j == 0)
    def _():
        acc_ref[...] = x_ref[...]
        up_tile(False)

    @pl.when(jnp.logical_and(j > 0, j < nj))
    def _():
        up_tile(True)

    @pl.when(j == nj)
    def _():
        y = acc_ref[...] + _dot(act_ref[...], wd_ref[...])
        if emit == "final":
            xo_ref[...] = _rms(y, gn_ref[...])
        else:
            xo_ref[...] = y
            if ho_ref is not None:
                ho_ref[...] = _rms(y, gn_ref[...]).astype(ho_ref.dtype)


def conv_ffn(h, x, w_up, cw, w_down, g_next, *, layer, seq_len, emit, tm=512, tf=512):
    T, D = x.shape
    F = w_down.shape[1]
    nj = F // tf
    up_j = lambda j: jnp.minimum(j, nj - 1)
    down_j = lambda j: jnp.maximum(j - 1, 0)
    kern = functools.partial(_ffn_kernel, tiles_per_seq=seq_len // tm, emit=emit)
    out_shape = [jax.ShapeDtypeStruct((T, D), F32)]
    out_specs = [pl.BlockSpec((tm, D), lambda i, j: (i, 0))]
    if emit == "x+h":
        out_shape.append(jax.ShapeDtypeStruct((T, D), BF16))
        out_specs.append(pl.BlockSpec((tm, D), lambda i, j: (i, 0)))
    res = pl.pallas_call(
        kern,
        out_shape=tuple(out_shape),
        grid=(T // tm, nj + 1),
        in_specs=[pl.BlockSpec((tm, D), lambda i, j: (i, 0)),
                  pl.BlockSpec((tm, D), lambda i, j: (i, 0)),
                  pl.BlockSpec((None, D, tf), lambda i, j: (layer, 0, up_j(j))),
                  pl.BlockSpec((None, D, tf), lambda i, j: (layer, 0, up_j(j) + nj)),
                  pl.BlockSpec((None, 4, tf), lambda i, j: (layer, 0, up_j(j))),
                  pl.BlockSpec((None, 4, tf), lambda i, j: (layer, 0, up_j(j) + nj)),
                  pl.BlockSpec((None, tf, D), lambda i, j: (layer, down_j(j), 0)),
                  pl.BlockSpec((1, D), lambda i, j: (0, 0))],
        out_specs=tuple(out_specs),
        scratch_shapes=[pltpu.VMEM((tm, D), F32),
                        pltpu.VMEM((nj, 2, CONV_HALO, tf), F32),
                        pltpu.VMEM((tm, tf), BF16)],
        compiler_params=_cparams(("arbitrary", "arbitrary")),
        name="conv_ffn",
    )(h, x, w_up, w_up, cw, cw, w_down, g_next.reshape(1, D))
    return res


def _pool_kernel(x_ref, gp_ref, w_ref, sc_ref, gn_ref, xo_ref, ho_ref, halo_ref,
                 *, tiles_per_seq):
    i = pl.program_id(0)
    tm, D = x_ref.shape
    G = len(POOL_WINDOWS)
    gd = D // G
    ti = i % tiles_per_seq
    x = x_ref[...]
    h = _rms(x, gp_ref[...])

    @pl.when(ti == 0)
    def _():
        halo_ref[...] = jnp.zeros_like(halo_ref)

    prev = halo_ref[...]
    halo_ref[...] = h[tm - POOL_HALO:, :]
    pos = ti * tm + lax.broadcasted_iota(jnp.int32, (tm, 1), 0)
    outs = []
    for g, win in enumerate(POOL_WINDOWS):
        hg = h[:, g * gd:(g + 1) * gd]
        s = jnp.concatenate([prev[:, g * gd:(g + 1) * gd], hg], axis=0)
        sh = 1
        while sh < win:
            s = s + pltpu.roll(s, sh, 0)
            sh *= 2
        wsum = s[POOL_HALO:, :]
        count = jnp.minimum(pos + 1, win).astype(F32)
        pooled = wsum / count - hg
        y = _dot(pooled.astype(BF16), w_ref[g])
        outs.append(x[:, g * gd:(g + 1) * gd] + y * sc_ref[:, g * gd:(g + 1) * gd])
    xn = jnp.concatenate(outs, axis=1)
    xo_ref[...] = xn
    ho_ref[...] = _rms(xn, gn_ref[...]).astype(ho_ref.dtype)


def pool_mixer(x, g_pool, w_pool, pool_scale, g_next, *, seq_len, tm=512):
    T, D = x.shape
    G, gd, _ = w_pool.shape
    kern = functools.partial(_pool_kernel, tiles_per_seq=seq_len // tm)
    return pl.pallas_call(
        kern,
        out_shape=(jax.ShapeDtypeStruct((T, D), F32), jax.ShapeDtypeStruct((T, D), BF16)),
        grid=(T // tm,),
        in_specs=[pl.BlockSpec((tm, D), lambda i: (i, 0)),
                  pl.BlockSpec((1, D), lambda i: (0, 0)),
                  pl.BlockSpec((G, gd, gd), lambda i: (0, 0, 0)),
                  pl.BlockSpec((1, D), lambda i: (0, 0)),
                  pl.BlockSpec((1, D), lambda i: (0, 0))],
        out_specs=(pl.BlockSpec((tm, D), lambda i: (i, 0)),
                   pl.BlockSpec((tm, D), lambda i: (i, 0))),
        scratch_shapes=[pltpu.VMEM((POOL_HALO, D), F32)],
        compiler_params=_cparams(("arbitrary",)),
        name="pool_mixer",
    )(x, g_pool.reshape(1, D), w_pool, pool_scale.reshape(1, D), g_next.reshape(1, D))


def _fox_cumsum_kernel(f_ref, b_ref, ct_ref, carry_ref, *, tiles_per_seq):
    i = pl.program_id(0)
    tc = f_ref.shape[0]

    @pl.when((i % tiles_per_seq) == 0)
    def _():
        carry_ref[...] = jnp.zeros_like(carry_ref)

    z = f_ref[...] + b_ref[...]
    lf = jnp.minimum(z, 0.0) - jnp.log1p(jnp.exp(-jnp.abs(z)))
    c = _cumsum_rows(_tril_ones(tc, BF16), lf) + carry_ref[...]
    carry_ref[...] = c[tc - 1:tc, :]
    ct_ref[...] = (c * LOG2E).T[:ct_ref.shape[0], :]


def fox_cumsum(misc, f_col_block, b_f_padded, n_heads, *, seq_len, tc=256):
    T = misc.shape[0]
    kern = functools.partial(_fox_cumsum_kernel, tiles_per_seq=seq_len // tc)
    return pl.pallas_call(
        kern,
        out_shape=jax.ShapeDtypeStruct((n_heads, T), F32),
        grid=(T // tc,),
        in_specs=[pl.BlockSpec((tc, LANES), lambda i: (i, f_col_block)),
                  pl.BlockSpec((1, LANES), lambda i: (0, 0))],
        out_specs=pl.BlockSpec((n_heads, tc), lambda i: (0, i)),
        scratch_shapes=[pltpu.VMEM((1, LANES), F32)],
        compiler_params=_cparams(("arbitrary",)),
        name="fox_cumsum",
    )(misc, b_f_padded)


def _fox_kernel(q_ref, k_ref, v_ref, ck_ref, o_ref, vaug_ref, m_ref, acc_ref, sa_ref, sb_ref,
                *, tk, nsplit):
    i = pl.program_id(2)
    tq, dh = q_ref.shape
    hq = tq // nsplit
    nslab = tk // LANES

    @pl.when(i == 0)
    def _():
        vaug_ref[:, :dh] = v_ref[...]
        vaug_ref[:, dh:] = jnp.ones((vaug_ref.shape[0], dh), BF16)

    m_ref[...] = jnp.full_like(m_ref, NEG_BIG)
    acc_ref[...] = jnp.zeros_like(acc_ref)

    rows = [slice(h * hq, (h + 1) * hq) for h in range(nsplit)]

    def logits(j, s_ref):
        off = pl.multiple_of(j * tk, tk)
        kj = k_ref[pl.ds(off, tk), :]
        ckj = ck_ref[0, :, pl.ds(off, tk)]
        for h in range(nsplit):
            s_ref[rows[h], :] = _dot_nt(q_ref[rows[h], :], kj) - ckj

    def softmax_pv(j, s_ref, masked):
        off = pl.multiple_of(j * tk, tk)
        vj = vaug_ref[pl.ds(off, tk), :]
        ss = [s_ref[rows[h], :] for h in range(nsplit)]
        if masked:
            c = lax.broadcasted_iota(jnp.int32, (hq, tk), 1)
            r = lax.broadcasted_iota(jnp.int32, (hq, tk), 0)
            ss = [jnp.where(c <= r + h * hq, ss[h], NEG_BIG) for h in range(nsplit)]
        ps, alphas = [], []
        for h in range(nsplit):
            slabs = [ss[h][:, k * LANES:(k + 1) * LANES] for k in range(nslab)]
            smax = functools.reduce(jnp.maximum, slabs)
            m_old = m_ref[rows[h], :]
            m_new = jnp.maximum(m_old, jnp.max(smax, axis=-1, keepdims=True))
            m_ref[rows[h], :] = m_new
            alphas.append(jnp.exp2(m_old - m_new))
            ps.append(jnp.concatenate([jnp.exp2(sl - m_new).astype(BF16) for sl in slabs], axis=1))
        for h in range(nsplit):
            alpha2 = jnp.concatenate([alphas[h], alphas[h]], axis=1)
            acc_ref[rows[h], :] = alpha2 * acc_ref[rows[h], :] + _dot(ps[h], vj)

    logits(0, sa_ref)

    def body(jj, carry):
        j = 2 * jj
        logits(j + 1, sb_ref)
        softmax_pv(j, sa_ref, False)
        logits(j + 2, sa_ref)
        softmax_pv(j + 1, sb_ref, False)
        return carry

    lax.fori_loop(0, i // 2, body, 0)

    @pl.when(i % 2 == 1)
    def _():
        logits(i, sb_ref)
        softmax_pv(i - 1, sa_ref, False)
        softmax_pv(i, sb_ref, True)

    @pl.when(i % 2 == 0)
    def _():
        softmax_pv(i, sa_ref, True)

    acc = acc_ref[...]
    o_ref[...] = (acc[:, :dh] / acc[:, dh:]).astype(o_ref.dtype)


def fox_attention(qkv, ck, *, batch, seq_len, n_heads, tq=512, nsplit=4):
    T = qkv.shape[0]
    nq = seq_len // tq
    dh = FOX_HEAD_DIM
    kern = functools.partial(_fox_kernel, tk=tq, nsplit=nsplit)
    return pl.pallas_call(
        kern,
        out_shape=jax.ShapeDtypeStruct((T, n_heads * dh), BF16),
        grid=(batch, n_heads, nq),
        in_specs=[pl.BlockSpec((tq, dh), lambda b, h, i: (b * nq + i, h)),
                  pl.BlockSpec((seq_len, dh), lambda b, h, i: (b, n_heads + h)),
                  pl.BlockSpec((seq_len, dh), lambda b, h, i: (b, 2 * n_heads + h)),
                  pl.BlockSpec((1, 1, seq_len), lambda b, h, i: (b * n_heads + h, 0, 0))],
        out_specs=pl.BlockSpec((tq, dh), lambda b, h, i: (b * nq + i, h)),
        scratch_shapes=[pltpu.VMEM((seq_len, 2 * dh), BF16), pltpu.VMEM((tq, dh), F32),
                        pltpu.VMEM((tq, 2 * dh), F32),
                        pltpu.VMEM((tq, tq), F32), pltpu.VMEM((tq, tq), F32)],
        compiler_params=_cparams(("parallel", "parallel", "arbitrary")),
        name="fox_attention",
    )(qkv, qkv, qkv, ck)


def _rwkv_kernel(rkv_ref, rkvp_ref, mi_ref, mip_ref, mu_rkv_ref, mu_mi_ref,
                 w0_ref, wdu_ref, a0_ref, wiu_ref, wgu_ref, kk_ref, ka_ref, rk_ref,
                 gnw_ref, gnb_ref, tri_ref, o_ref, state_ref, y_ref):
    c = pl.program_id(0)
    B, C, _ = rkv_ref.shape
    W = w0_ref.shape[1]
    N = RWKV_HEAD_DIM
    H = W // N

    @pl.when(c == 0)
    def _():
        state_ref[...] = jnp.zeros_like(state_ref)

    row0 = lax.broadcasted_iota(jnp.int32, (C, 1), 0) == 0
    tri = tri_ref[...]
    lane = lax.broadcasted_iota(jnp.int32, (2 * C, LANES), 1)
    lo_half = lane < N
    hi_half = lane >= N
    lane_c = lax.broadcasted_iota(jnp.int32, (C, LANES), 1)
    lo_half_c = lane_c < N
    hi_half_c = lane_c >= N

    def head_sums(x):
        out = []
        for k in range(W // LANES):
            xs = x[:, k * LANES:(k + 1) * LANES]
            lo = jnp.sum(jnp.where(lo_half_c, xs, 0.0), axis=-1, keepdims=True)
            hi = jnp.sum(jnp.where(hi_half_c, xs, 0.0), axis=-1, keepdims=True)
            out.append(jnp.where(lo_half_c, jnp.broadcast_to(lo, xs.shape),
                                 jnp.broadcast_to(hi, xs.shape)))
        return jnp.concatenate(out, axis=1)

    def token_shift_lerp(cur, prev8, mu_ref):
        last = jnp.where(c == 0, 0.0, prev8[SUBLANES - 1:SUBLANES, :])
        shifted = jnp.where(row0, last, pltpu.roll(cur, 1, 0))
        return cur + (shifted - cur) * mu_ref[...]

    r2 = lax.broadcasted_iota(jnp.int32, (2 * C, 2 * C), 0)
    c2 = lax.broadcasted_iota(jnp.int32, (2 * C, 2 * C), 1)
    rr = jnp.where(r2 >= C, r2 - C, r2)
    cc = jnp.where(c2 >= C, c2 - C, c2)
    amask = cc + jnp.where(r2 >= C, 1, 0) <= rr
    ri = lax.broadcasted_iota(jnp.int32, (C, C), 0)
    ci = lax.broadcasted_iota(jnp.int32, (C, C), 1)
    eye = (ri == ci).astype(F32)

    probs = []
    gates = []
    bonus = []
    for b in range(B):
        rw = token_shift_lerp(rkv_ref[b], rkvp_ref[b], mu_rkv_ref)
        mi = token_shift_lerp(mi_ref[b], mip_ref[b], mu_mi_ref)
        r = rw[:, :W]
        kb = rw[:, W:2 * W]
        vb = rw[:, 2 * W:]
        wd = mi[:, :LANES]
        ad = mi[:, LANES:2 * LANES]
        gd = mi[:, 2 * LANES:]

        lw = -DECAY_SCALE * jax.nn.sigmoid(
            w0_ref[...] + _dot(jnp.tanh(wd).astype(BF16), wdu_ref[...]))
        a = jax.nn.sigmoid(a0_ref[...] + _dot(ad.astype(BF16), wiu_ref[...]))
        gates.append(_dot(jax.nn.sigmoid(gd).astype(BF16), wgu_ref[...]))
        kkraw = kb * kk_ref[...]
        kmod = kb * (1.0 + (a - 1.0) * ka_ref[...])
        kk = kkraw / jnp.maximum(jnp.sqrt(head_sums(kkraw * kkraw)), 1e-12)
        bk = a * kk
        bonus.append(head_sums(r * kmod * rk_ref[...]) * vb)

        G = _cumsum_rows(tri, lw)
        gam = jnp.exp(G)
        gam_inv = jnp.exp(-G)
        gam_c = gam[C - 1:C, :]
        gam_tail = jnp.exp(G[C - 1:C, :] - G)
        L1 = jnp.concatenate([r * gam, kk * jnp.exp(G - lw)], axis=0)
        R1 = jnp.concatenate([kmod * gam_inv, bk * gam_inv], axis=0).astype(BF16)
        KB = jnp.concatenate([kmod * gam_tail, bk * gam_tail], axis=0).astype(BF16)

        for hd in range(H):
            slab = slice((hd // 2) * LANES, (hd // 2 + 1) * LANES)
            mine = lo_half if hd % 2 == 0 else hi_half
            probs.append(dict(
                b=b, hd=hd, sl=slice(hd * N, (hd + 1) * N), slab=slab, mine=mine,
                pair=b * (H // 2) + hd // 2, v=vb[:, hd * N:(hd + 1) * N],
                L1=jnp.where(mine, L1[:, slab], 0.0).astype(BF16),
                R1=R1[:, slab], KB=KB[:, slab], gam_c=gam_c[:, slab]))

    pair_state = [state_ref[q] for q in range(B * H // 2)]
    pair_state_bf = [s.astype(BF16) for s in pair_state]
    for p in probs:
        p["A"] = jnp.where(amask, _dot_nt(p["L1"], p["R1"]), 0.0)
    for p in probs:
        p["RH"] = _dot_nt(p["L1"], pair_state_bf[p["pair"]])
    for p in probs:
        nm = p["A"][C:, C:]
        p["X"] = eye - nm
        nmb = nm.astype(BF16)
        p["P"] = _dot(nmb, nmb)
    span = 2
    while span < C:
        for p in probs:
            p["Pb"] = p["P"].astype(BF16)
            p["X"] = p["X"] + _dot(p["X"].astype(BF16), p["Pb"])
        span *= 2
        if span < C:
            for p in probs:
                p["P"] = _dot(p["Pb"], p["Pb"])
    for p in probs:
        p["Z"] = p["RH"][C:, :] + _dot(p["A"][C:, :C].astype(BF16), p["v"].astype(BF16))
    for p in probs:
        p["U"] = _dot(p["X"].astype(BF16), p["Z"].astype(BF16))
    for p in probs:
        p["VU"] = jnp.concatenate([p["v"], -p["U"]], axis=0).astype(BF16)
        p["Y"] = p["RH"][:C, :] + _dot(p["A"][:C, :].astype(BF16), p["VU"])
    for p in probs:
        p["dS"] = lax.dot_general(p["VU"], p["KB"], (((0,), (0,)), ((), ())),
                                  preferred_element_type=F32)
    for p_even, p_odd in zip(probs[0::2], probs[1::2]):
        q = p_even["pair"]
        state_ref[q] = pair_state[q] * p_even["gam_c"] + jnp.where(
            lax.broadcasted_iota(jnp.int32, (N, LANES), 1) < N, p_even["dS"], p_odd["dS"])
    for p in probs:
        y_ref[p["b"], :, p["sl"]] = p["Y"]

    for b in range(B):
        y = y_ref[b]
        d = y - head_sums(y) * (1.0 / N)
        var = head_sums(d * d) * (1.0 / N)
        yn = d * lax.rsqrt(var + GN_EPS) * gnw_ref[...] + gnb_ref[...]
        o_ref[b] = ((yn + bonus[b]) * gates[b]).astype(o_ref.dtype)


def rwkv7_mixer(rkv, misc, mu_rkv, mu_mi, w0, wdu, a0, wiu, wgu, k_k, k_a, r_k, gn_w, gn_b,
                *, batch, seq_len):
    T, W3 = rkv.shape
    W = W3 // 3
    C = RWKV_CHUNK
    nc = seq_len // C
    MI = 4 * LANES
    H = W // RWKV_HEAD_DIM
    rkv3 = rkv.reshape(batch, seq_len, W3)
    misc3 = misc.reshape(batch, seq_len, misc.shape[1])
    row = lambda v: v.reshape(1, -1)
    full = lambda shape: pl.BlockSpec(shape, lambda c: (0,) * len(shape))
    cur_blk = lambda c: (0, c, 0)
    prev_blk = lambda c: (0, jnp.maximum(c * (C // SUBLANES) - 1, 0), 0)
    out = pl.pallas_call(
        _rwkv_kernel,
        out_shape=jax.ShapeDtypeStruct((batch, seq_len, W), BF16),
        grid=(nc,),
        in_specs=[pl.BlockSpec((batch, C, W3), cur_blk),
                  pl.BlockSpec((batch, SUBLANES, W3), prev_blk),
                  pl.BlockSpec((batch, C, MI), cur_blk),
                  pl.BlockSpec((batch, SUBLANES, MI), prev_blk),
                  full((1, W3)), full((1, MI)),
                  full((1, W)), full((LANES, W)), full((1, W)), full((LANES, W)),
                  full((2 * LANES, W)), full((1, W)), full((1, W)), full((1, W)),
                  full((1, W)), full((1, W)), full((C, C))],
        out_specs=pl.BlockSpec((batch, C, W), cur_blk),
        scratch_shapes=[pltpu.VMEM((batch * H // 2, RWKV_HEAD_DIM, 2 * RWKV_HEAD_DIM), F32),
                        pltpu.VMEM((batch, C, W), F32)],
        compiler_params=_cparams(("arbitrary",)),
        name="rwkv7_mixer",
    )(rkv3, rkv3, misc3, misc3, row(mu_rkv), row(mu_mi), row(w0), wdu, row(a0), wiu, wgu,
      row(k_k), row(k_a), row(r_k), row(gn_w), row(gn_b), jnp.tril(jnp.ones((C, C), BF16)))
    return out.reshape(T, W)


def _pad_cols(w, n):
    return jnp.pad(w, ((0, 0), (0, n - w.shape[1])))


def _pad_rows(w, n):
    return jnp.pad(w, ((0, n - w.shape[0]), (0, 0)))


def _fox_rwkv_layer(x, h, p, g_next, *, batch, seq_len):
    D = x.shape[1]
    n_fox = FOX_HEADS
    fox_w = n_fox * FOX_HEAD_DIM
    rw_w = D - fox_w
    w_in = p["w_in"]
    fox_cols = 3 * fox_w + n_fox
    scale = FOX_HEAD_DIM ** -0.5

    w_q = w_in[:, :fox_w] * (scale * LOG2E)
    w_qkv = jnp.concatenate([w_q, w_in[:, fox_w:3 * fox_w]], axis=1).astype(BF16)
    w_f = w_in[:, 3 * fox_w:fox_cols]
    w_rkv = w_in[:, fox_cols:fox_cols + 3 * rw_w].astype(BF16)
    o = fox_cols + 3 * rw_w
    w_wd = w_in[:, o:o + DECAY_RANK]
    w_ad = w_in[:, o + DECAY_RANK:o + DECAY_RANK + ICLR_RANK]
    w_gd = w_in[:, o + DECAY_RANK + ICLR_RANK:]
    w_misc = jnp.concatenate([_pad_cols(w_wd, LANES), _pad_cols(w_ad, LANES), w_gd,
                              _pad_cols(w_f, LANES)], axis=1).astype(BF16)
    mu = p["shift_mu"]
    mu_rkv = mu[:3 * rw_w]
    mu_wd = mu[3 * rw_w:3 * rw_w + DECAY_RANK]
    mu_ad = mu[3 * rw_w + DECAY_RANK:3 * rw_w + DECAY_RANK + ICLR_RANK]
    mu_gd = mu[3 * rw_w + DECAY_RANK + ICLR_RANK:]
    pad1 = lambda v: jnp.pad(v, (0, LANES - v.shape[0]))
    mu_mi = jnp.concatenate([pad1(mu_wd), pad1(mu_ad), mu_gd])

    qkv = matmul(h, w_qkv, BF16, name="proj_qkv")
    rkv = matmul(h, w_rkv, F32, name="proj_rkv")
    misc = matmul(h, w_misc, F32, tn=5 * LANES, name="proj_misc")

    b_f = jnp.pad(p["b_f"], (0, LANES - n_fox)).reshape(1, LANES)
    ct = fox_cumsum(misc, 4, b_f, n_fox, seq_len=seq_len)
    ck = ct.reshape(n_fox, batch, seq_len).transpose(1, 0, 2).reshape(batch * n_fox, 1, seq_len)
    o_a = fox_attention(qkv, ck, batch=batch, seq_len=seq_len, n_heads=n_fox)

    o_b = rwkv7_mixer(
        rkv, misc, mu_rkv, mu_mi, p["w0"],
        _pad_rows(p["w_decay_up"], LANES).astype(BF16), p["a0"],
        _pad_rows(p["w_iclr_up"], LANES).astype(BF16), p["w_gate_up"].astype(BF16),
        p["k_k"], p["k_a"], p["r_k"].reshape(-1), p["gn_w"], p["gn_b"],
        batch=batch, seq_len=seq_len)

    w_out = p["w_out"].astype(BF16)
    return out_projection(o_a, o_b, x, w_out[:fox_w], w_out[fox_w:], g_next)


def kernel(x, mix_norm, w_in, b_f, shift_mu, w0, w_decay_up, a0, w_iclr_up, w_gate_up, k_k, k_a, r_k, gn_w, gn_b, w_out, pool_norm, w_pool, pool_scale, ffn_norm, w_ffn_up, conv_w, conv_b, w_ffn_down, final_norm):
    B, S, D = x.shape
    depth = ffn_norm.shape[0]
    xt = x.reshape(B * S, D)
    h = rmsnorm_bf16(xt, mix_norm[0])
    w_up_bf = w_ffn_up.astype(BF16)
    w_down_bf = w_ffn_down.astype(BF16)
    conv_taps = jnp.concatenate([conv_w, conv_b[:, None, :]], axis=1)
    e = 0
    o = 0
    for layer in range(depth):
        if layer % 2 == 0:
            p = dict(w_in=w_in[e], b_f=b_f[e], shift_mu=shift_mu[e], w0=w0[e],
                     w_decay_up=w_decay_up[e], a0=a0[e], w_iclr_up=w_iclr_up[e],
                     w_gate_up=w_gate_up[e], k_k=k_k[e], k_a=k_a[e], r_k=r_k[e],
                     gn_w=gn_w[e], gn_b=gn_b[e], w_out=w_out[e])
            xt, h = _fox_rwkv_layer(xt, h, p, ffn_norm[layer], batch=B, seq_len=S)
            e += 1
        else:
            xt, h = pool_mixer(xt, pool_norm[o], w_pool[o].astype(BF16), pool_scale[o],
                               ffn_norm[layer], seq_len=S)
            o += 1
        last = layer == depth - 1
        if last:
            g_next, emit = final_norm, "final"
        elif (layer + 1) % 2 == 0:
            g_next, emit = mix_norm[e], "x+h"
        else:
            g_next, emit = pool_norm[o], "x"
        res = conv_ffn(h, xt, w_up_bf, conv_taps, w_down_bf, g_next, layer=layer,
                       seq_len=S, emit=emit)
        if emit == "x+h":
            xt, h = res
        else:
            xt, h = res[0], None
    return xt.reshape(B, S, D)
```

```python
import functools

import jax
import jax.numpy as jnp
from jax import lax
from jax.experimental import pallas as pl
from jax.experimental.pallas import tpu as pltpu

F32 = jnp.float32
BF16 = jnp.bfloat16

RMS_EPS = 1e-6
GN_EPS = 64e-5
FOX_HEADS = 8
FOX_HEAD_DIM = 128
RWKV_HEAD_DIM = 64
POOL_WINDOWS = (2, 4, 8, 16)
CONV_WIDTH = 3
DECAY_RANK = 96
ICLR_RANK = 96
GATE_RANK = 256

LANES = 128
SUBLANES = 8
VMEM_LIMIT = 56 * 1024 * 1024

RWKV_CHUNK = 64
RWKV_SKEW = 2
RWKV_GROUPS = 2
POOL_HALO = 16
CONV_HALO = 8
FFN_TILE = 512
NEG_BIG = -1e30
LOG2E = 1.4426950408889634
DECAY_SCALE = 0.6065306597126334


def _cparams(sem):
    return pltpu.CompilerParams(dimension_semantics=sem, vmem_limit_bytes=VMEM_LIMIT)


def _rms(x, g):
    ms = jnp.mean(x * x, axis=-1, keepdims=True)
    return x * lax.rsqrt(ms + RMS_EPS) * g


def _dot(a, b):
    return jnp.dot(a, b, preferred_element_type=F32)


def _dot_nt(a, b):
    return lax.dot_general(a, b, (((1,), (1,)), ((), ())), preferred_element_type=F32)


def _split3(x):
    hi = x.astype(BF16)
    r1 = x - hi.astype(F32)
    mid = r1.astype(BF16)
    lo = (r1 - mid.astype(F32)).astype(BF16)
    return hi, mid, lo


def _cumsum_rows(tri, x):
    hi, mid, lo = _split3(x)
    return _dot(tri, hi) + _dot(tri, mid) + _dot(tri, lo)


def _tril_ones(n, dtype):
    r = lax.broadcasted_iota(jnp.int32, (n, n), 0)
    c = lax.broadcasted_iota(jnp.int32, (n, n), 1)
    return (c <= r).astype(dtype)


def _softplus(x):
    return jnp.maximum(x, 0.0) + jnp.log1p(jnp.exp(-jnp.abs(x)))


def _rms_kernel(x_ref, g_ref, o_ref):
    o_ref[...] = _rms(x_ref[...], g_ref[...]).astype(o_ref.dtype)


def rmsnorm_bf16(x, g, tm=512):
    T, D = x.shape
    return pl.pallas_call(
        _rms_kernel,
        out_shape=jax.ShapeDtypeStruct((T, D), BF16),
        grid=(T // tm,),
        in_specs=[pl.BlockSpec((tm, D), lambda i: (i, 0)),
                  pl.BlockSpec((1, D), lambda i: (0, 0))],
        out_specs=pl.BlockSpec((tm, D), lambda i: (i, 0)),
        compiler_params=_cparams(("parallel",)),
        name="rmsnorm",
    )(x, g.reshape(1, D))


def _mm_kernel(a_ref, w_ref, o_ref):
    o_ref[...] = _dot(a_ref[...], w_ref[...]).astype(o_ref.dtype)


def matmul(a, w, out_dtype, tm=1024, tn=1024, name="matmul"):
    M, K = a.shape
    N = w.shape[1]
    tm = min(tm, M)
    tn = min(tn, N)
    return pl.pallas_call(
        _mm_kernel,
        out_shape=jax.ShapeDtypeStruct((M, N), out_dtype),
        grid=(M // tm, pl.cdiv(N, tn)),
        in_specs=[pl.BlockSpec((tm, K), lambda i, j: (i, 0)),
                  pl.BlockSpec((K, tn), lambda i, j: (0, j))],
        out_specs=pl.BlockSpec((tm, tn), lambda i, j: (i, j)),
        compiler_params=_cparams(("parallel", "parallel")),
        name=name,
    )(a, w)


def _outproj_kernel(oa_ref, ob_ref, x_ref, wa_ref, wb_ref, g_ref, xo_ref, ho_ref):
    y = x_ref[...] + _dot(oa_ref[...], wa_ref[...]) + _dot(ob_ref[...], wb_ref[...])
    xo_ref[...] = y
    ho_ref[...] = _rms(y, g_ref[...]).astype(ho_ref.dtype)


def out_projection(o_a, o_b, x, w_a, w_b, g_next, tm=512):
    T, D = x.shape
    Ka, Kb = o_a.shape[1], o_b.shape[1]
    return pl.pallas_call(
        _outproj_kernel,
        out_shape=(jax.ShapeDtypeStruct((T, D), F32), jax.ShapeDtypeStruct((T, D), BF16)),
        grid=(T // tm,),
        in_specs=[pl.BlockSpec((tm, Ka), lambda i: (i, 0)),
                  pl.BlockSpec((tm, Kb), lambda i: (i, 0)),
                  pl.BlockSpec((tm, D), lambda i: (i, 0)),
                  pl.BlockSpec((Ka, D), lambda i: (0, 0)),
                  pl.BlockSpec((Kb, D), lambda i: (0, 0)),
                  pl.BlockSpec((1, D), lambda i: (0, 0))],
        out_specs=(pl.BlockSpec((tm, D), lambda i: (i, 0)),
                   pl.BlockSpec((tm, D), lambda i: (i, 0))),
        compiler_params=_cparams(("parallel",)),
        name="out_projection",
    )(o_a, o_b, x, w_a, w_b, g_next.reshape(1, D))


def _ffn_kernel(h_ref, x_ref, wu_ref, cw_ref, wd_ref, gn_ref, *rest, tiles_per_seq, emit):
    if emit == "x+h":
        xo_ref, ho_ref, acc_ref, halo_ref = rest
    else:
        xo_ref, acc_ref, halo_ref = rest
        ho_ref = None
    i = pl.program_id(0)
    j = pl.program_id(1)
    nj = pl.num_programs(1)
    tm = h_ref.shape[0]
    tf = wd_ref.shape[0]

    @pl.when(j == 0)
    def _():
        acc_ref[...] = x_ref[...]

    @pl.when((i % tiles_per_seq) == 0)
    def _():
        halo_ref[j] = jnp.zeros(halo_ref.shape[1:], F32)

    u = _dot(h_ref[...], wu_ref[...])
    prev = halo_ref[j]
    halo_ref[j] = u[tm - CONV_HALO:, :]
    cw = cw_ref[...]
    first_rows = lax.broadcasted_iota(jnp.int32, (CONV_HALO, 2 * tf), 0)

    def shifted(k):
        top = jnp.where(first_rows < k, pltpu.roll(prev, k, 0), pltpu.roll(u[:CONV_HALO], k, 0))
        return jnp.concatenate([top, pltpu.roll(u, k, 0)[CONV_HALO:]], axis=0)

    uc = shifted(2) * cw[0:1, :] + shifted(1) * cw[1:2, :] + u * cw[2:3, :] + cw[3:4, :]
    gate = uc[:, :tf]
    val = uc[:, tf:]
    act = (gate * (0.5 * jnp.tanh(0.5 * gate) + 0.5) * val).astype(BF16)
    acc_ref[...] += _dot(act, wd_ref[...])

    @pl.when(j == nj - 1)
    def _():
        y = acc_ref[...]
        if emit == "final":
            xo_ref[...] = _rms(y, gn_ref[...])
        else:
            xo_ref[...] = y
            if ho_ref is not None:
                ho_ref[...] = _rms(y, gn_ref[...]).astype(ho_ref.dtype)


def _interleave_gate_val(w, tf):
    lead = w.shape[:-1]
    nj = w.shape[-1] // (2 * tf)
    w = w.reshape(*lead, 2, nj, tf)
    return jnp.swapaxes(w, -3, -2).reshape(*lead, 2 * nj * tf)


def conv_ffn(h, x, w_up, cw, w_down, g_next, *, layer, seq_len, emit, tm=512, tf=FFN_TILE):
    T, D = x.shape
    F = w_down.shape[1]
    nj = F // tf
    kern = functools.partial(_ffn_kernel, tiles_per_seq=seq_len // tm, emit=emit)
    out_shape = [jax.ShapeDtypeStruct((T, D), F32)]
    out_specs = [pl.BlockSpec((tm, D), lambda i, j: (i, 0))]
    if emit == "x+h":
        out_shape.append(jax.ShapeDtypeStruct((T, D), BF16))
        out_specs.append(pl.BlockSpec((tm, D), lambda i, j: (i, 0)))
    res = pl.pallas_call(
        kern,
        out_shape=tuple(out_shape),
        grid=(T // tm, nj),
        in_specs=[pl.BlockSpec((tm, D), lambda i, j: (i, 0)),
                  pl.BlockSpec((tm, D), lambda i, j: (i, 0)),
                  pl.BlockSpec((None, D, 2 * tf), lambda i, j: (layer, 0, j)),
                  pl.BlockSpec((None, 4, 2 * tf), lambda i, j: (layer, 0, j)),
                  pl.BlockSpec((None, tf, D), lambda i, j: (layer, j, 0)),
                  pl.BlockSpec((1, D), lambda i, j: (0, 0))],
        out_specs=tuple(out_specs),
        scratch_shapes=[pltpu.VMEM((tm, D), F32),
                        pltpu.VMEM((nj, CONV_HALO, 2 * tf), F32)],
        compiler_params=_cparams(("arbitrary", "arbitrary")),
        name="conv_ffn",
    )(h, x, w_up, cw, w_down, g_next.reshape(1, D))
    return res


def _pool_kernel(x_ref, gp_ref, w_ref, sc_ref, gn_ref, xo_ref, ho_ref, halo_ref,
                 *, tiles_per_seq):
    i = pl.program_id(0)
    tm, D = x_ref.shape
    G = len(POOL_WINDOWS)
    gd = D // G
    ti = i % tiles_per_seq
    x = x_ref[...]
    h = _rms(x, gp_ref[...])

    @pl.when(ti == 0)
    def _():
        halo_ref[...] = jnp.zeros_like(halo_ref)

    prev = halo_ref[...]
    halo_ref[...] = h[tm - POOL_HALO:, :]
    pos = ti * tm + lax.broadcasted_iota(jnp.int32, (tm, 1), 0)
    outs = []
    for g, win in enumerate(POOL_WINDOWS):
        hg = h[:, g * gd:(g + 1) * gd]
        s = jnp.concatenate([prev[:, g * gd:(g + 1) * gd], hg], axis=0)
        sh = 1
        while sh < win:
            s = s + pltpu.roll(s, sh, 0)
            sh *= 2
        wsum = s[POOL_HALO:, :]
        count = jnp.minimum(pos + 1, win).astype(F32)
        pooled = wsum / count - hg
        y = _dot(pooled.astype(BF16), w_ref[g])
        outs.append(x[:, g * gd:(g + 1) * gd] + y * sc_ref[:, g * gd:(g + 1) * gd])
    xn = jnp.concatenate(outs, axis=1)
    xo_ref[...] = xn
    ho_ref[...] = _rms(xn, gn_ref[...]).astype(ho_ref.dtype)


def pool_mixer(x, g_pool, w_pool, pool_scale, g_next, *, seq_len, tm=512):
    T, D = x.shape
    G, gd, _ = w_pool.shape
    kern = functools.partial(_pool_kernel, tiles_per_seq=seq_len // tm)
    return pl.pallas_call(
        kern,
        out_shape=(jax.ShapeDtypeStruct((T, D), F32), jax.ShapeDtypeStruct((T, D), BF16)),
        grid=(T // tm,),
        in_specs=[pl.BlockSpec((tm, D), lambda i: (i, 0)),
                  pl.BlockSpec((1, D), lambda i: (0, 0)),
                  pl.BlockSpec((G, gd, gd), lambda i: (0, 0, 0)),
                  pl.BlockSpec((1, D), lambda i: (0, 0)),
                  pl.BlockSpec((1, D), lambda i: (0, 0))],
        out_specs=(pl.BlockSpec((tm, D), lambda i: (i, 0)),
                   pl.BlockSpec((tm, D), lambda i: (i, 0))),
        scratch_shapes=[pltpu.VMEM((POOL_HALO, D), F32)],
        compiler_params=_cparams(("arbitrary",)),
        name="pool_mixer",
    )(x, g_pool.reshape(1, D), w_pool, pool_scale.reshape(1, D), g_next.reshape(1, D))


def _fox_cumsum_kernel(f_ref, b_ref, ct_ref, carry_ref, *, tiles_per_seq):
    i = pl.program_id(0)
    tc = f_ref.shape[0]

    @pl.when((i % tiles_per_seq) == 0)
    def _():
        carry_ref[...] = jnp.zeros_like(carry_ref)

    z = f_ref[...] + b_ref[...]
    lf = jnp.minimum(z, 0.0) - jnp.log1p(jnp.exp(-jnp.abs(z)))
    c = _cumsum_rows(_tril_ones(tc, BF16), lf) + carry_ref[...]
    carry_ref[...] = c[tc - 1:tc, :]
    ct_ref[...] = (c * LOG2E).T[:ct_ref.shape[0], :]


def fox_cumsum(misc, f_col_block, b_f_padded, n_heads, *, seq_len, tc=256):
    T = misc.shape[0]
    kern = functools.partial(_fox_cumsum_kernel, tiles_per_seq=seq_len // tc)
    return pl.pallas_call(
        kern,
        out_shape=jax.ShapeDtypeStruct((n_heads, T), F32),
        grid=(T // tc,),
        in_specs=[pl.BlockSpec((tc, LANES), lambda i: (i, f_col_block)),
                  pl.BlockSpec((1, LANES), lambda i: (0, 0))],
        out_specs=pl.BlockSpec((n_heads, tc), lambda i: (0, i)),
        scratch_shapes=[pltpu.VMEM((1, LANES), F32)],
        compiler_params=_cparams(("arbitrary",)),
        name="fox_cumsum",
    )(misc, b_f_padded)


def _fox_kernel(q_ref, k_ref, v_ref, ck_ref, o_ref, vaug_ref, m_ref, acc_ref, sa_ref, sb_ref,
                *, tk, nsplit):
    i = pl.program_id(2)
    tq, dh = q_ref.shape
    hq = tq // nsplit
    nslab = tk // LANES

    @pl.when(i == 0)
    def _():
        vaug_ref[:, :dh] = v_ref[...]
        vaug_ref[:, dh:] = jnp.ones((vaug_ref.shape[0], dh), BF16)

    m_ref[...] = jnp.full_like(m_ref, NEG_BIG)
    acc_ref[...] = jnp.zeros_like(acc_ref)

    rows = [slice(h * hq, (h + 1) * hq) for h in range(nsplit)]

    def logits(j, s_ref):
        off = pl.multiple_of(j * tk, tk)
        kj = k_ref[pl.ds(off, tk), :]
        ckj = ck_ref[0, :, pl.ds(off, tk)]
        for h in range(nsplit):
            s_ref[rows[h], :] = _dot_nt(q_ref[rows[h], :], kj) - ckj

    def softmax_pv(j, s_ref, masked):
        off = pl.multiple_of(j * tk, tk)
        vj = vaug_ref[pl.ds(off, tk), :]
        ss = [s_ref[rows[h], :] for h in range(nsplit)]
        if masked:
            c = lax.broadcasted_iota(jnp.int32, (hq, tk), 1)
            r = lax.broadcasted_iota(jnp.int32, (hq, tk), 0)
            ss = [jnp.where(c <= r + h * hq, ss[h], NEG_BIG) for h in range(nsplit)]
        ps, alphas = [], []
        for h in range(nsplit):
            slabs = [ss[h][:, k * LANES:(k + 1) * LANES] for k in range(nslab)]
            smax = functools.reduce(jnp.maximum, slabs)
            m_old = m_ref[rows[h], :]
            m_new = jnp.maximum(m_old, jnp.max(smax, axis=-1, keepdims=True))
            m_ref[rows[h], :] = m_new
            alphas.append(jnp.exp2(m_old - m_new))
            ps.append(jnp.concatenate([jnp.exp2(sl - m_new).astype(BF16) for sl in slabs], axis=1))
        for h in range(nsplit):
            alpha2 = jnp.concatenate([alphas[h], alphas[h]], axis=1)
            acc_ref[rows[h], :] = alpha2 * acc_ref[rows[h], :] + _dot(ps[h], vj)

    logits(0, sa_ref)

    def body(jj, carry):
        j = 2 * jj
        logits(j + 1, sb_ref)
        softmax_pv(j, sa_ref, False)
        logits(j + 2, sa_ref)
        softmax_pv(j + 1, sb_ref, False)
        return carry

    lax.fori_loop(0, i // 2, body, 0)

    @pl.when(i % 2 == 1)
    def _():
        logits(i, sb_ref)
        softmax_pv(i - 1, sa_ref, False)
        softmax_pv(i, sb_ref, True)

    @pl.when(i % 2 == 0)
    def _():
        softmax_pv(i, sa_ref, True)

    acc = acc_ref[...]
    o_ref[...] = (acc[:, :dh] / acc[:, dh:]).astype(o_ref.dtype)


def fox_attention(qkv, ck, *, batch, seq_len, n_heads, tq=512, nsplit=2):
    T = qkv.shape[0]
    nq = seq_len // tq
    dh = FOX_HEAD_DIM
    kern = functools.partial(_fox_kernel, tk=tq, nsplit=nsplit)
    return pl.pallas_call(
        kern,
        out_shape=jax.ShapeDtypeStruct((T, n_heads * dh), BF16),
        grid=(batch, n_heads, nq),
        in_specs=[pl.BlockSpec((tq, dh), lambda b, h, i: (b * nq + i, h)),
                  pl.BlockSpec((seq_len, dh), lambda b, h, i: (b, n_heads + h)),
                  pl.BlockSpec((seq_len, dh), lambda b, h, i: (b, 2 * n_heads + h)),
                  pl.BlockSpec((1, 1, seq_len), lambda b, h, i: (b * n_heads + h, 0, 0))],
        out_specs=pl.BlockSpec((tq, dh), lambda b, h, i: (b * nq + i, h)),
        scratch_shapes=[pltpu.VMEM((seq_len, 2 * dh), BF16), pltpu.VMEM((tq, dh), F32),
                        pltpu.VMEM((tq, 2 * dh), F32),
                        pltpu.VMEM((tq, tq), F32), pltpu.VMEM((tq, tq), F32)],
        compiler_params=_cparams(("parallel", "parallel", "arbitrary")),
        name="fox_attention",
    )(qkv, qkv, qkv, ck)


def _rwkv_kernel(rkv_ref, rkvp_ref, mi_ref, mip_ref, mu_rkv_ref, mu_mi_ref,
                 w0_ref, wdu_ref, a0_ref, wiu_ref, wgu_ref, kk_ref, ka_ref, rk_ref,
                 gnw_ref, gnb_ref, tri_ref, o_ref, state_ref, y_ref):
    c = pl.program_id(0)
    B, C, _ = rkv_ref.shape
    W = w0_ref.shape[1]
    N = RWKV_HEAD_DIM
    H = W // N

    @pl.when(c == 0)
    def _():
        state_ref[...] = jnp.zeros_like(state_ref)

    row0 = lax.broadcasted_iota(jnp.int32, (C, 1), 0) == 0
    tri = tri_ref[...]
    lane = lax.broadcasted_iota(jnp.int32, (2 * C, LANES), 1)
    lo_half = lane < N
    hi_half = lane >= N
    lane_c = lax.broadcasted_iota(jnp.int32, (C, LANES), 1)
    lo_half_c = lane_c < N
    hi_half_c = lane_c >= N

    def head_sums(x):
        out = []
        for k in range(W // LANES):
            xs = x[:, k * LANES:(k + 1) * LANES]
            lo = jnp.sum(jnp.where(lo_half_c, xs, 0.0), axis=-1, keepdims=True)
            hi = jnp.sum(jnp.where(hi_half_c, xs, 0.0), axis=-1, keepdims=True)
            out.append(jnp.where(lo_half_c, jnp.broadcast_to(lo, xs.shape),
                                 jnp.broadcast_to(hi, xs.shape)))
        return jnp.concatenate(out, axis=1)

    def token_shift_lerp(cur, prev8, mu_ref):
        last = jnp.where(c == 0, 0.0, prev8[SUBLANES - 1:SUBLANES, :])
        shifted = jnp.where(row0, last, pltpu.roll(cur, 1, 0))
        return cur + (shifted - cur) * mu_ref[...]

    r2 = lax.broadcasted_iota(jnp.int32, (2 * C, 2 * C), 0)
    c2 = lax.broadcasted_iota(jnp.int32, (2 * C, 2 * C), 1)
    rr = jnp.where(r2 >= C, r2 - C, r2)
    cc = jnp.where(c2 >= C, c2 - C, c2)
    amask = cc + jnp.where(r2 >= C, 1, 0) <= rr
    ri = lax.broadcasted_iota(jnp.int32, (C, C), 0)
    ci = lax.broadcasted_iota(jnp.int32, (C, C), 1)
    eye = (ri == ci).astype(F32)

    probs = []
    gates = []
    bonus = []
    for b in range(B):
        rw = token_shift_lerp(rkv_ref[b], rkvp_ref[b], mu_rkv_ref)
        mi = token_shift_lerp(mi_ref[b], mip_ref[b], mu_mi_ref)
        r = rw[:, :W]
        kb = rw[:, W:2 * W]
        vb = rw[:, 2 * W:]
        wd = mi[:, :LANES]
        ad = mi[:, LANES:2 * LANES]
        gd = mi[:, 2 * LANES:]

        lw = -DECAY_SCALE * jax.nn.sigmoid(
            w0_ref[...] + _dot(jnp.tanh(wd).astype(BF16), wdu_ref[...]))
        a = jax.nn.sigmoid(a0_ref[...] + _dot(ad.astype(BF16), wiu_ref[...]))
        gates.append(_dot(jax.nn.sigmoid(gd).astype(BF16), wgu_ref[...]))
        kkraw = kb * kk_ref[...]
        kmod = kb * (1.0 + (a - 1.0) * ka_ref[...])
        kk = kkraw / jnp.maximum(jnp.sqrt(head_sums(kkraw * kkraw)), 1e-12)
        bk = a * kk
        bonus.append(head_sums(r * kmod * rk_ref[...]) * vb)

        G = _cumsum_rows(tri, lw)
        gam = jnp.exp(G)
        gam_inv = jnp.exp(-G)
        gam_c = gam[C - 1:C, :]
        gam_tail = jnp.exp(G[C - 1:C, :] - G)
        L1 = jnp.concatenate([r * gam, kk * jnp.exp(G - lw)], axis=0)
        R1 = jnp.concatenate([bk * gam_inv, kmod * gam_inv], axis=0).astype(BF16)
        KB = jnp.concatenate([bk * gam_tail, kmod * gam_tail], axis=0).astype(BF16)

        for hd in range(H):
            slab = slice((hd // 2) * LANES, (hd // 2 + 1) * LANES)
            mine = lo_half if hd % 2 == 0 else hi_half
            probs.append(dict(
                b=b, hd=hd, sl=slice(hd * N, (hd + 1) * N), slab=slab, mine=mine,
                pair=b * (H // 2) + hd // 2, v=vb[:, hd * N:(hd + 1) * N],
                L1=jnp.where(mine, L1[:, slab], 0.0).astype(BF16),
                R1=R1[:, slab], KB=KB[:, slab], gam_c=gam_c[:, slab]))

    def solve(group):
        pair_state = {p["pair"]: state_ref[p["pair"]] for p in group[0::2]}
        pair_state_bf = {q: s.astype(BF16) for q, s in pair_state.items()}
        for p in group:
            p["A"] = jnp.where(amask, _dot_nt(p["L1"], p["R1"]), 0.0)
        yield
        for p in group:
            p["RH"] = _dot_nt(p["L1"], pair_state_bf[p["pair"]])
        for p in group:
            nm = p["A"][C:, :C]
            p["X"] = eye - nm
            nmb = nm.astype(BF16)
            p["P"] = _dot(nmb, nmb)
        yield
        span = 2
        while span < C:
            for p in group:
                p["Pb"] = p["P"].astype(BF16)
                p["X"] = p["X"] + _dot(p["X"].astype(BF16), p["Pb"])
            span *= 2
            if span < C:
                for p in group:
                    p["P"] = _dot(p["Pb"], p["Pb"])
            yield
        for p in group:
            zv = jnp.concatenate([jnp.zeros_like(p["v"]), p["v"]], axis=0).astype(BF16)
            p["Z"] = p["RH"][C:, :] + _dot(p["A"][C:, :].astype(BF16), zv)
        yield
        for p in group:
            p["U"] = _dot(p["X"].astype(BF16), p["Z"].astype(BF16))
        yield
        for p in group:
            p["VU"] = jnp.concatenate([-p["U"], p["v"]], axis=0).astype(BF16)
            p["Y"] = p["RH"][:C, :] + _dot(p["A"][:C, :].astype(BF16), p["VU"])
        for p in group:
            p["dS"] = lax.dot_general(p["VU"], p["KB"], (((0,), (0,)), ((), ())),
                                      preferred_element_type=F32)
        yield
        for p_even, p_odd in zip(group[0::2], group[1::2]):
            q = p_even["pair"]
            state_ref[q] = pair_state[q] * p_even["gam_c"] + jnp.where(
                lax.broadcasted_iota(jnp.int32, (N, LANES), 1) < N, p_even["dS"], p_odd["dS"])
        for p in group:
            y_ref[p["b"], :, p["sl"]] = p["Y"]

    per_group = len(probs) // RWKV_GROUPS
    waiting = [solve(probs[g * per_group:(g + 1) * per_group]) for g in range(RWKV_GROUPS)]
    running = []
    rounds = 0
    while waiting or running:
        if waiting and rounds % RWKV_SKEW == 0:
            running.append(waiting.pop(0))
        for gen in list(running):
            if next(gen, "done") == "done":
                running.remove(gen)
        rounds += 1

    for b in range(B):
        y = y_ref[b]
        d = y - head_sums(y) * (1.0 / N)
        var = head_sums(d * d) * (1.0 / N)
        yn = d * lax.rsqrt(var + GN_EPS) * gnw_ref[...] + gnb_ref[...]
        o_ref[b] = ((yn + bonus[b]) * gates[b]).astype(o_ref.dtype)


def rwkv7_mixer(rkv, misc, mu_rkv, mu_mi, w0, wdu, a0, wiu, wgu, k_k, k_a, r_k, gn_w, gn_b,
                *, batch, seq_len):
    T, W3 = rkv.shape
    W = W3 // 3
    C = RWKV_CHUNK
    nc = seq_len // C
    MI = 4 * LANES
    H = W // RWKV_HEAD_DIM
    rkv3 = rkv.reshape(batch, seq_len, W3)
    misc3 = misc.reshape(batch, seq_len, misc.shape[1])
    row = lambda v: v.reshape(1, -1)
    full = lambda shape: pl.BlockSpec(shape, lambda c: (0,) * len(shape))
    cur_blk = lambda c: (0, c, 0)
    prev_blk = lambda c: (0, jnp.maximum(c * (C // SUBLANES) - 1, 0), 0)
    out = pl.pallas_call(
        _rwkv_kernel,
        out_shape=jax.ShapeDtypeStruct((batch, seq_len, W), BF16),
        grid=(nc,),
        in_specs=[pl.BlockSpec((batch, C, W3), cur_blk),
                  pl.BlockSpec((batch, SUBLANES, W3), prev_blk),
                  pl.BlockSpec((batch, C, MI), cur_blk),
                  pl.BlockSpec((batch, SUBLANES, MI), prev_blk),
                  full((1, W3)), full((1, MI)),
                  full((1, W)), full((LANES, W)), full((1, W)), full((LANES, W)),
                  full((2 * LANES, W)), full((1, W)), full((1, W)), full((1, W)),
                  full((1, W)), full((1, W)), full((C, C))],
        out_specs=pl.BlockSpec((batch, C, W), cur_blk),
        scratch_shapes=[pltpu.VMEM((batch * H // 2, RWKV_HEAD_DIM, 2 * RWKV_HEAD_DIM), F32),
                        pltpu.VMEM((batch, C, W), F32)],
        compiler_params=_cparams(("arbitrary",)),
        name="rwkv7_mixer",
    )(rkv3, rkv3, misc3, misc3, row(mu_rkv), row(mu_mi), row(w0), wdu, row(a0), wiu, wgu,
      row(k_k), row(k_a), row(r_k), row(gn_w), row(gn_b), jnp.tril(jnp.ones((C, C), BF16)))
    return out.reshape(T, W)


def _pad_cols(w, n):
    return jnp.pad(w, ((0, 0), (0, n - w.shape[1])))


def _pad_rows(w, n):
    return jnp.pad(w, ((0, n - w.shape[0]), (0, 0)))


def _fox_rwkv_layer(x, h, p, g_next, *, batch, seq_len):
    D = x.shape[1]
    n_fox = FOX_HEADS
    fox_w = n_fox * FOX_HEAD_DIM
    rw_w = D - fox_w
    w_in = p["w_in"]
    fox_cols = 3 * fox_w + n_fox
    scale = FOX_HEAD_DIM ** -0.5

    w_q = w_in[:, :fox_w] * (scale * LOG2E)
    w_qkv = jnp.concatenate([w_q, w_in[:, fox_w:3 * fox_w]], axis=1).astype(BF16)
    w_f = w_in[:, 3 * fox_w:fox_cols]
    w_rkv = w_in[:, fox_cols:fox_cols + 3 * rw_w].astype(BF16)
    o = fox_cols + 3 * rw_w
    w_wd = w_in[:, o:o + DECAY_RANK]
    w_ad = w_in[:, o + DECAY_RANK:o + DECAY_RANK + ICLR_RANK]
    w_gd = w_in[:, o + DECAY_RANK + ICLR_RANK:]
    w_misc = jnp.concatenate([_pad_cols(w_wd, LANES), _pad_cols(w_ad, LANES), w_gd,
                              _pad_cols(w_f, LANES)], axis=1).astype(BF16)
    mu = p["shift_mu"]
    mu_rkv = mu[:3 * rw_w]
    mu_wd = mu[3 * rw_w:3 * rw_w + DECAY_RANK]
    mu_ad = mu[3 * rw_w + DECAY_RANK:3 * rw_w + DECAY_RANK + ICLR_RANK]
    mu_gd = mu[3 * rw_w + DECAY_RANK + ICLR_RANK:]
    pad1 = lambda v: jnp.pad(v, (0, LANES - v.shape[0]))
    mu_mi = jnp.concatenate([pad1(mu_wd), pad1(mu_ad), mu_gd])

    qkv = matmul(h, w_qkv, BF16, name="proj_qkv")
    rkv = matmul(h, w_rkv, F32, name="proj_rkv")
    misc = matmul(h, w_misc, F32, tn=5 * LANES, name="proj_misc")

    b_f = jnp.pad(p["b_f"], (0, LANES - n_fox)).reshape(1, LANES)
    ct = fox_cumsum(misc, 4, b_f, n_fox, seq_len=seq_len)
    ck = ct.reshape(n_fox, batch, seq_len).transpose(1, 0, 2).reshape(batch * n_fox, 1, seq_len)
    o_a = fox_attention(qkv, ck, batch=batch, seq_len=seq_len, n_heads=n_fox)

    o_b = rwkv7_mixer(
        rkv, misc, mu_rkv, mu_mi, p["w0"],
        _pad_rows(p["w_decay_up"], LANES).astype(BF16), p["a0"],
        _pad_rows(p["w_iclr_up"], LANES).astype(BF16), p["w_gate_up"].astype(BF16),
        p["k_k"], p["k_a"], p["r_k"].reshape(-1), p["gn_w"], p["gn_b"],
        batch=batch, seq_len=seq_len)

    w_out = p["w_out"].astype(BF16)
    return out_projection(o_a, o_b, x, w_out[:fox_w], w_out[fox_w:], g_next)


def kernel(x, mix_norm, w_in, b_f, shift_mu, w0, w_decay_up, a0, w_iclr_up, w_gate_up, k_k, k_a, r_k, gn_w, gn_b, w_out, pool_norm, w_pool, pool_scale, ffn_norm, w_ffn_up, conv_w, conv_b, w_ffn_down, final_norm):
    B, S, D = x.shape
    depth = ffn_norm.shape[0]
    xt = x.reshape(B * S, D)
    h = rmsnorm_bf16(xt, mix_norm[0])
    w_up_bf = _interleave_gate_val(w_ffn_up.astype(BF16), FFN_TILE)
    w_down_bf = w_ffn_down.astype(BF16)
    conv_taps = _interleave_gate_val(
        jnp.concatenate([conv_w, conv_b[:, None, :]], axis=1), FFN_TILE)
    e = 0
    o = 0
    for layer in range(depth):
        if layer % 2 == 0:
            p = dict(w_in=w_in[e], b_f=b_f[e], shift_mu=shift_mu[e], w0=w0[e],
                     w_decay_up=w_decay_up[e], a0=a0[e], w_iclr_up=w_iclr_up[e],
                     w_gate_up=w_gate_up[e], k_k=k_k[e], k_a=k_a[e], r_k=r_k[e],
                     gn_w=gn_w[e], gn_b=gn_b[e], w_out=w_out[e])
            xt, h = _fox_rwkv_layer(xt, h, p, ffn_norm[layer], batch=B, seq_len=S)
            e += 1
        else:
            xt, h = pool_mixer(xt, pool_norm[o], w_pool[o].astype(BF16), pool_scale[o],
                               ffn_norm[layer], seq_len=S)
            o += 1
        last = layer == depth - 1
        if last:
            g_next, emit = final_norm, "final"
        elif (layer + 1) % 2 == 0:
            g_next, emit = mix_norm[e], "x+h"
        else:
            g_next, emit = pool_norm[o], "x"
        res = conv_ffn(h, xt, w_up_bf, conv_taps, w_down_bf, g_next, layer=layer,
                       seq_len=S, emit=emit)
        if emit == "x+h":
            xt, h = res
        else:
            xt, h = res[0], None
    return xt.reshape(B, S, D)
```

```python
import functools

import jax
import jax.numpy as jnp
from jax import lax
from jax.experimental import pallas as pl
from jax.experimental.pallas import tpu as pltpu

F32 = jnp.float32
BF16 = jnp.bfloat16

RMS_EPS = 1e-6
GN_EPS = 64e-5
FOX_HEADS = 8
FOX_HEAD_DIM = 128
RWKV_HEAD_DIM = 64
POOL_WINDOWS = (2, 4, 8, 16)
CONV_WIDTH = 3
DECAY_RANK = 96
ICLR_RANK = 96
GATE_RANK = 256

LANES = 128
SUBLANES = 8
VMEM_LIMIT = 56 * 1024 * 1024

RWKV_CHUNK = 64
RWKV_SKEW = 2
RWKV_GROUPS = 2
POOL_HALO = 16
CONV_HALO = 8
FFN_TILE = 512
NEG_BIG = -1e30
LOG2E = 1.4426950408889634
DECAY_SCALE = 0.6065306597126334


def _cparams(sem):
    return pltpu.CompilerParams(dimension_semantics=sem, vmem_limit_bytes=VMEM_LIMIT)


def _rms(x, g):
    ms = jnp.mean(x * x, axis=-1, keepdims=True)
    return x * lax.rsqrt(ms + RMS_EPS) * g


def _dot(a, b):
    return jnp.dot(a, b, preferred_element_type=F32)


def _dot_nt(a, b):
    return lax.dot_general(a, b, (((1,), (1,)), ((), ())), preferred_element_type=F32)


def _split3(x):
    hi = x.astype(BF16)
    r1 = x - hi.astype(F32)
    mid = r1.astype(BF16)
    lo = (r1 - mid.astype(F32)).astype(BF16)
    return hi, mid, lo


def _cumsum_rows(tri, x):
    hi, mid, lo = _split3(x)
    return _dot(tri, hi) + _dot(tri, mid) + _dot(tri, lo)


def _tril_ones(n, dtype):
    r = lax.broadcasted_iota(jnp.int32, (n, n), 0)
    c = lax.broadcasted_iota(jnp.int32, (n, n), 1)
    return (c <= r).astype(dtype)


def _softplus(x):
    return jnp.maximum(x, 0.0) + jnp.log1p(jnp.exp(-jnp.abs(x)))


def _rms_kernel(x_ref, g_ref, o_ref):
    o_ref[...] = _rms(x_ref[...], g_ref[...]).astype(o_ref.dtype)


def rmsnorm_bf16(x, g, tm=512):
    T, D = x.shape
    return pl.pallas_call(
        _rms_kernel,
        out_shape=jax.ShapeDtypeStruct((T, D), BF16),
        grid=(T // tm,),
        in_specs=[pl.BlockSpec((tm, D), lambda i: (i, 0)),
                  pl.BlockSpec((1, D), lambda i: (0, 0))],
        out_specs=pl.BlockSpec((tm, D), lambda i: (i, 0)),
        compiler_params=_cparams(("parallel",)),
        name="rmsnorm",
    )(x, g.reshape(1, D))


def _mm_kernel(a_ref, w_ref, o_ref):
    o_ref[...] = _dot(a_ref[...], w_ref[...]).astype(o_ref.dtype)


def matmul(a, w, out_dtype, tm=1024, tn=1024, name="matmul"):
    M, K = a.shape
    N = w.shape[1]
    tm = min(tm, M)
    tn = min(tn, N)
    return pl.pallas_call(
        _mm_kernel,
        out_shape=jax.ShapeDtypeStruct((M, N), out_dtype),
        grid=(M // tm, pl.cdiv(N, tn)),
        in_specs=[pl.BlockSpec((tm, K), lambda i, j: (i, 0)),
                  pl.BlockSpec((K, tn), lambda i, j: (0, j))],
        out_specs=pl.BlockSpec((tm, tn), lambda i, j: (i, j)),
        compiler_params=_cparams(("parallel", "parallel")),
        name=name,
    )(a, w)


def _outproj_kernel(oa_ref, ob_ref, x_ref, wa_ref, wb_ref, g_ref, xo_ref, ho_ref):
    y = x_ref[...] + _dot(oa_ref[...], wa_ref[...]) + _dot(ob_ref[...], wb_ref[...])
    xo_ref[...] = y
    ho_ref[...] = _rms(y, g_ref[...]).astype(ho_ref.dtype)


def out_projection(o_a, o_b, x, w_a, w_b, g_next, tm=512):
    T, D = x.shape
    Ka, Kb = o_a.shape[1], o_b.shape[1]
    return pl.pallas_call(
        _outproj_kernel,
        out_shape=(jax.ShapeDtypeStruct((T, D), F32), jax.ShapeDtypeStruct((T, D), BF16)),
        grid=(T // tm,),
        in_specs=[pl.BlockSpec((tm, Ka), lambda i: (i, 0)),
                  pl.BlockSpec((tm, Kb), lambda i: (i, 0)),
                  pl.BlockSpec((tm, D), lambda i: (i, 0)),
                  pl.BlockSpec((Ka, D), lambda i: (0, 0)),
                  pl.BlockSpec((Kb, D), lambda i: (0, 0)),
                  pl.BlockSpec((1, D), lambda i: (0, 0))],
        out_specs=(pl.BlockSpec((tm, D), lambda i: (i, 0)),
                   pl.BlockSpec((tm, D), lambda i: (i, 0))),
        compiler_params=_cparams(("parallel",)),
        name="out_projection",
    )(o_a, o_b, x, w_a, w_b, g_next.reshape(1, D))


def _ffn_kernel(h_ref, x_ref, wg_ref, wv_ref, cwg_ref, cwv_ref, wd_ref, gn_ref, *rest,
                tiles_per_seq, emit):
    if emit == "x+h":
        xo_ref, ho_ref, acc_ref, halo_ref = rest
    else:
        xo_ref, acc_ref, halo_ref = rest
        ho_ref = None
    i = pl.program_id(0)
    j = pl.program_id(1)
    nj = pl.num_programs(1)
    tm = h_ref.shape[0]
    tf = wd_ref.shape[0]

    @pl.when(j == 0)
    def _():
        acc_ref[...] = x_ref[...]

    @pl.when((i % tiles_per_seq) == 0)
    def _():
        halo_ref[j] = jnp.zeros(halo_ref.shape[1:], F32)

    h = h_ref[...]
    u = jnp.concatenate([_dot(h, wg_ref[...]), _dot(h, wv_ref[...])], axis=1)
    prev = halo_ref[j]
    halo_ref[j] = u[tm - CONV_HALO:, :]
    cw = jnp.concatenate([cwg_ref[...], cwv_ref[...]], axis=1)
    first_rows = lax.broadcasted_iota(jnp.int32, (CONV_HALO, 2 * tf), 0)

    def shifted(k):
        top = jnp.where(first_rows < k, pltpu.roll(prev, k, 0), pltpu.roll(u[:CONV_HALO], k, 0))
        return jnp.concatenate([top, pltpu.roll(u, k, 0)[CONV_HALO:]], axis=0)

    uc = shifted(2) * cw[0:1, :] + shifted(1) * cw[1:2, :] + u * cw[2:3, :] + cw[3:4, :]
    gate = uc[:, :tf]
    val = uc[:, tf:]
    act = (gate * (0.5 * jnp.tanh(0.5 * gate) + 0.5) * val).astype(BF16)
    acc_ref[...] += _dot(act, wd_ref[...])

    @pl.when(j == nj - 1)
    def _():
        y = acc_ref[...]
        if emit == "final":
            xo_ref[...] = _rms(y, gn_ref[...])
        else:
            xo_ref[...] = y
            if ho_ref is not None:
                ho_ref[...] = _rms(y, gn_ref[...]).astype(ho_ref.dtype)


def conv_ffn(h, x, w_up, cw, w_down, g_next, *, layer, seq_len, emit, tm=512, tf=FFN_TILE):
    T, D = x.shape
    F = w_down.shape[1]
    nj = F // tf
    kern = functools.partial(_ffn_kernel, tiles_per_seq=seq_len // tm, emit=emit)
    out_shape = [jax.ShapeDtypeStruct((T, D), F32)]
    out_specs = [pl.BlockSpec((tm, D), lambda i, j: (i, 0))]
    if emit == "x+h":
        out_shape.append(jax.ShapeDtypeStruct((T, D), BF16))
        out_specs.append(pl.BlockSpec((tm, D), lambda i, j: (i, 0)))
    res = pl.pallas_call(
        kern,
        out_shape=tuple(out_shape),
        grid=(T // tm, nj),
        in_specs=[pl.BlockSpec((tm, D), lambda i, j: (i, 0)),
                  pl.BlockSpec((tm, D), lambda i, j: (i, 0)),
                  pl.BlockSpec((None, D, tf), lambda i, j: (layer, 0, j)),
                  pl.BlockSpec((None, D, tf), lambda i, j: (layer, 0, j + nj)),
                  pl.BlockSpec((None, 4, tf), lambda i, j: (layer, 0, j)),
                  pl.BlockSpec((None, 4, tf), lambda i, j: (layer, 0, j + nj)),
                  pl.BlockSpec((None, tf, D), lambda i, j: (layer, j, 0)),
                  pl.BlockSpec((1, D), lambda i, j: (0, 0))],
        out_specs=tuple(out_specs),
        scratch_shapes=[pltpu.VMEM((tm, D), F32),
                        pltpu.VMEM((nj, CONV_HALO, 2 * tf), F32)],
        compiler_params=_cparams(("arbitrary", "arbitrary")),
        name="conv_ffn",
    )(h, x, w_up, w_up, cw, cw, w_down, g_next.reshape(1, D))
    return res


def _pool_kernel(x_ref, gp_ref, w_ref, sc_ref, gn_ref, xo_ref, ho_ref, halo_ref,
                 *, tiles_per_seq):
    i = pl.program_id(0)
    tm, D = x_ref.shape
    G = len(POOL_WINDOWS)
    gd = D // G
    ti = i % tiles_per_seq
    x = x_ref[...]
    h = _rms(x, gp_ref[...])

    @pl.when(ti == 0)
    def _():
        halo_ref[...] = jnp.zeros_like(halo_ref)

    prev = halo_ref[...]
    halo_ref[...] = h[tm - POOL_HALO:, :]
    pos = ti * tm + lax.broadcasted_iota(jnp.int32, (tm, 1), 0)
    outs = []
    for g, win in enumerate(POOL_WINDOWS):
        hg = h[:, g * gd:(g + 1) * gd]
        s = jnp.concatenate([prev[:, g * gd:(g + 1) * gd], hg], axis=0)
        sh = 1
        while sh < win:
            s = s + pltpu.roll(s, sh, 0)
            sh *= 2
        wsum = s[POOL_HALO:, :]
        count = jnp.minimum(pos + 1, win).astype(F32)
        pooled = wsum / count - hg
        y = _dot(pooled.astype(BF16), w_ref[g])
        outs.append(x[:, g * gd:(g + 1) * gd] + y * sc_ref[:, g * gd:(g + 1) * gd])
    xn = jnp.concatenate(outs, axis=1)
    xo_ref[...] = xn
    ho_ref[...] = _rms(xn, gn_ref[...]).astype(ho_ref.dtype)


def pool_mixer(x, g_pool, w_pool, pool_scale, g_next, *, seq_len, tm=512):
    T, D = x.shape
    G, gd, _ = w_pool.shape
    kern = functools.partial(_pool_kernel, tiles_per_seq=seq_len // tm)
    return pl.pallas_call(
        kern,
        out_shape=(jax.ShapeDtypeStruct((T, D), F32), jax.ShapeDtypeStruct((T, D), BF16)),
        grid=(T // tm,),
        in_specs=[pl.BlockSpec((tm, D), lambda i: (i, 0)),
                  pl.BlockSpec((1, D), lambda i: (0, 0)),
                  pl.BlockSpec((G, gd, gd), lambda i: (0, 0, 0)),
                  pl.BlockSpec((1, D), lambda i: (0, 0)),
                  pl.BlockSpec((1, D), lambda i: (0, 0))],
        out_specs=(pl.BlockSpec((tm, D), lambda i: (i, 0)),
                   pl.BlockSpec((tm, D), lambda i: (i, 0))),
        scratch_shapes=[pltpu.VMEM((POOL_HALO, D), F32)],
        compiler_params=_cparams(("arbitrary",)),
        name="pool_mixer",
    )(x, g_pool.reshape(1, D), w_pool, pool_scale.reshape(1, D), g_next.reshape(1, D))


def _fox_cumsum_kernel(f_ref, b_ref, ct_ref, carry_ref, *, tiles_per_seq):
    i = pl.program_id(0)
    tc = f_ref.shape[0]

    @pl.when((i % tiles_per_seq) == 0)
    def _():
        carry_ref[...] = jnp.zeros_like(carry_ref)

    z = f_ref[...] + b_ref[...]
    lf = jnp.minimum(z, 0.0) - jnp.log1p(jnp.exp(-jnp.abs(z)))
    c = _cumsum_rows(_tril_ones(tc, BF16), lf) + carry_ref[...]
    carry_ref[...] = c[tc - 1:tc, :]
    ct_ref[...] = (c * LOG2E).T[:ct_ref.shape[0], :]


def fox_cumsum(misc, f_col_block, b_f_padded, n_heads, *, seq_len, tc=256):
    T = misc.shape[0]
    kern = functools.partial(_fox_cumsum_kernel, tiles_per_seq=seq_len // tc)
    return pl.pallas_call(
        kern,
        out_shape=jax.ShapeDtypeStruct((n_heads, T), F32),
        grid=(T // tc,),
        in_specs=[pl.BlockSpec((tc, LANES), lambda i: (i, f_col_block)),
                  pl.BlockSpec((1, LANES), lambda i: (0, 0))],
        out_specs=pl.BlockSpec((n_heads, tc), lambda i: (0, i)),
        scratch_shapes=[pltpu.VMEM((1, LANES), F32)],
        compiler_params=_cparams(("arbitrary",)),
        name="fox_cumsum",
    )(misc, b_f_padded)


def _fox_kernel(q_ref, k_ref, v_ref, ck_ref, o_ref, vaug_ref, m_ref, acc_ref, sa_ref, sb_ref,
                *, tk, nsplit):
    i = pl.program_id(2)
    tq, dh = q_ref.shape
    hq = tq // nsplit
    nslab = tk // LANES

    @pl.when(i == 0)
    def _():
        vaug_ref[:, :dh] = v_ref[...]
        vaug_ref[:, dh:] = jnp.ones((vaug_ref.shape[0], dh), BF16)

    m_ref[...] = jnp.full_like(m_ref, NEG_BIG)
    acc_ref[...] = jnp.zeros_like(acc_ref)

    rows = [slice(h * hq, (h + 1) * hq) for h in range(nsplit)]

    def logits(j, s_ref):
        off = pl.multiple_of(j * tk, tk)
        kj = k_ref[pl.ds(off, tk), :]
        ckj = ck_ref[0, :, pl.ds(off, tk)]
        for h in range(nsplit):
            s_ref[rows[h], :] = _dot_nt(q_ref[rows[h], :], kj) - ckj

    def softmax_pv(j, s_ref, masked):
        off = pl.multiple_of(j * tk, tk)
        vj = vaug_ref[pl.ds(off, tk), :]
        ss = [s_ref[rows[h], :] for h in range(nsplit)]
        if masked:
            c = lax.broadcasted_iota(jnp.int32, (hq, tk), 1)
            r = lax.broadcasted_iota(jnp.int32, (hq, tk), 0)
            ss = [jnp.where(c <= r + h * hq, ss[h], NEG_BIG) for h in range(nsplit)]
        ps, alphas = [], []
        for h in range(nsplit):
            slabs = [ss[h][:, k * LANES:(k + 1) * LANES] for k in range(nslab)]
            smax = functools.reduce(jnp.maximum, slabs)
            m_old = m_ref[rows[h], :]
            m_new = jnp.maximum(m_old, jnp.max(smax, axis=-1, keepdims=True))
            m_ref[rows[h], :] = m_new
            alphas.append(jnp.exp2(m_old - m_new))
            ps.append(jnp.concatenate([jnp.exp2(sl - m_new).astype(BF16) for sl in slabs], axis=1))
        for h in range(nsplit):
            alpha2 = jnp.concatenate([alphas[h], alphas[h]], axis=1)
            acc_ref[rows[h], :] = alpha2 * acc_ref[rows[h], :] + _dot(ps[h], vj)

    logits(0, sa_ref)

    def body(jj, carry):
        j = 2 * jj
        logits(j + 1, sb_ref)
        softmax_pv(j, sa_ref, False)
        logits(j + 2, sa_ref)
        softmax_pv(j + 1, sb_ref, False)
        return carry

    lax.fori_loop(0, i // 2, body, 0)

    @pl.when(i % 2 == 1)
    def _():
        logits(i, sb_ref)
        softmax_pv(i - 1, sa_ref, False)
        softmax_pv(i, sb_ref, True)

    @pl.when(i % 2 == 0)
    def _():
        softmax_pv(i, sa_ref, True)

    acc = acc_ref[...]
    o_ref[...] = (acc[:, :dh] / acc[:, dh:]).astype(o_ref.dtype)


def fox_attention(qkv, ck, *, batch, seq_len, n_heads, tq=512, nsplit=2):
    T = qkv.shape[0]
    nq = seq_len // tq
    dh = FOX_HEAD_DIM
    kern = functools.partial(_fox_kernel, tk=tq, nsplit=nsplit)
    return pl.pallas_call(
        kern,
        out_shape=jax.ShapeDtypeStruct((T, n_heads * dh), BF16),
        grid=(batch, n_heads, nq),
        in_specs=[pl.BlockSpec((tq, dh), lambda b, h, i: (b * nq + i, h)),
                  pl.BlockSpec((seq_len, dh), lambda b, h, i: (b, n_heads + h)),
                  pl.BlockSpec((seq_len, dh), lambda b, h, i: (b, 2 * n_heads + h)),
                  pl.BlockSpec((1, 1, seq_len), lambda b, h, i: (b * n_heads + h, 0, 0))],
        out_specs=pl.BlockSpec((tq, dh), lambda b, h, i: (b * nq + i, h)),
        scratch_shapes=[pltpu.VMEM((seq_len, 2 * dh), BF16), pltpu.VMEM((tq, dh), F32),
                        pltpu.VMEM((tq, 2 * dh), F32),
                        pltpu.VMEM((tq, tq), F32), pltpu.VMEM((tq, tq), F32)],
        compiler_params=_cparams(("parallel", "parallel", "arbitrary")),
        name="fox_attention",
    )(qkv, qkv, qkv, ck)


def _rwkv_kernel(rkv_ref, rkvp_ref, mi_ref, mip_ref, mu_rkv_ref, mu_mi_ref,
                 w0_ref, wdu_ref, a0_ref, wiu_ref, wgu_ref, kk_ref, ka_ref, rk_ref,
                 gnw_ref, gnb_ref, tri_ref, o_ref, state_ref, y_ref):
    c = pl.program_id(0)
    B, C, _ = rkv_ref.shape
    W = w0_ref.shape[1]
    N = RWKV_HEAD_DIM
    H = W // N

    @pl.when(c == 0)
    def _():
        state_ref[...] = jnp.zeros_like(state_ref)

    row0 = lax.broadcasted_iota(jnp.int32, (C, 1), 0) == 0
    tri = tri_ref[...]
    lane = lax.broadcasted_iota(jnp.int32, (2 * C, LANES), 1)
    lo_half = lane < N
    hi_half = lane >= N
    lane_c = lax.broadcasted_iota(jnp.int32, (C, LANES), 1)
    lo_half_c = lane_c < N
    hi_half_c = lane_c >= N

    def head_sums(x):
        out = []
        for k in range(W // LANES):
            xs = x[:, k * LANES:(k + 1) * LANES]
            lo = jnp.sum(jnp.where(lo_half_c, xs, 0.0), axis=-1, keepdims=True)
            hi = jnp.sum(jnp.where(hi_half_c, xs, 0.0), axis=-1, keepdims=True)
            out.append(jnp.where(lo_half_c, jnp.broadcast_to(lo, xs.shape),
                                 jnp.broadcast_to(hi, xs.shape)))
        return jnp.concatenate(out, axis=1)

    def token_shift_lerp(cur, prev8, mu_ref):
        last = jnp.where(c == 0, 0.0, prev8[SUBLANES - 1:SUBLANES, :])
        shifted = jnp.where(row0, last, pltpu.roll(cur, 1, 0))
        return cur + (shifted - cur) * mu_ref[...]

    r2 = lax.broadcasted_iota(jnp.int32, (2 * C, 2 * C), 0)
    c2 = lax.broadcasted_iota(jnp.int32, (2 * C, 2 * C), 1)
    rr = jnp.where(r2 >= C, r2 - C, r2)
    cc = jnp.where(c2 >= C, c2 - C, c2)
    amask = cc + jnp.where(r2 >= C, 1, 0) <= rr
    ri = lax.broadcasted_iota(jnp.int32, (C, C), 0)
    ci = lax.broadcasted_iota(jnp.int32, (C, C), 1)
    eye = (ri == ci).astype(F32)

    probs = []
    gates = []
    bonus = []
    for b in range(B):
        rw = token_shift_lerp(rkv_ref[b], rkvp_ref[b], mu_rkv_ref)
        mi = token_shift_lerp(mi_ref[b], mip_ref[b], mu_mi_ref)
        r = rw[:, :W]
        kb = rw[:, W:2 * W]
        vb = rw[:, 2 * W:]
        wd = mi[:, :LANES]
        ad = mi[:, LANES:2 * LANES]
        gd = mi[:, 2 * LANES:]

        lw = -DECAY_SCALE * jax.nn.sigmoid(
            w0_ref[...] + _dot(jnp.tanh(wd).astype(BF16), wdu_ref[...]))
        a = jax.nn.sigmoid(a0_ref[...] + _dot(ad.astype(BF16), wiu_ref[...]))
        gates.append(_dot(jax.nn.sigmoid(gd).astype(BF16), wgu_ref[...]))
        kkraw = kb * kk_ref[...]
        kmod = kb * (1.0 + (a - 1.0) * ka_ref[...])
        kk = kkraw / jnp.maximum(jnp.sqrt(head_sums(kkraw * kkraw)), 1e-12)
        bk = a * kk
        bonus.append(head_sums(r * kmod * rk_ref[...]) * vb)

        G = _cumsum_rows(tri, lw)
        gam = jnp.exp(G)
        gam_inv = jnp.exp(-G)
        gam_c = gam[C - 1:C, :]
        gam_tail = jnp.exp(G[C - 1:C, :] - G)
        L1 = jnp.concatenate([r * gam, kk * jnp.exp(G - lw)], axis=0)
        R1 = jnp.concatenate([bk * gam_inv, kmod * gam_inv], axis=0).astype(BF16)
        KB = jnp.concatenate([bk * gam_tail, kmod * gam_tail], axis=0).astype(BF16)

        for hd in range(H):
            slab = slice((hd // 2) * LANES, (hd // 2 + 1) * LANES)
            mine = lo_half if hd % 2 == 0 else hi_half
            probs.append(dict(
                b=b, hd=hd, sl=slice(hd * N, (hd + 1) * N), slab=slab, mine=mine,
                pair=b * (H // 2) + hd // 2, v=vb[:, hd * N:(hd + 1) * N],
                L1=jnp.where(mine, L1[:, slab], 0.0).astype(BF16),
                R1=R1[:, slab], KB=KB[:, slab], gam_c=gam_c[:, slab]))

    def solve(group):
        pair_state = {p["pair"]: state_ref[p["pair"]] for p in group[0::2]}
        pair_state_bf = {q: s.astype(BF16) for q, s in pair_state.items()}
        for p in group:
            p["A"] = jnp.where(amask, _dot_nt(p["L1"], p["R1"]), 0.0)
        yield
        for p in group:
            p["RH"] = _dot_nt(p["L1"], pair_state_bf[p["pair"]])
        for p in group:
            nm = p["A"][C:, :C]
            p["X"] = eye - nm
            nmb = nm.astype(BF16)
            p["P"] = _dot(nmb, nmb)
        yield
        span = 2
        while span < C:
            for p in group:
                p["Pb"] = p["P"].astype(BF16)
                p["X"] = p["X"] + _dot(p["X"].astype(BF16), p["Pb"])
            span *= 2
            if span < C:
                for p in group:
                    p["P"] = _dot(p["Pb"], p["Pb"])
            yield
        for p in group:
            zv = jnp.concatenate([jnp.zeros_like(p["v"]), p["v"]], axis=0).astype(BF16)
            p["Z"] = p["RH"][C:, :] + _dot(p["A"][C:, :].astype(BF16), zv)
        yield
        for p in group:
            p["U"] = _dot(p["X"].astype(BF16), p["Z"].astype(BF16))
        yield
        for p in group:
            p["VU"] = jnp.concatenate([-p["U"], p["v"]], axis=0).astype(BF16)
            p["Y"] = p["RH"][:C, :] + _dot(p["A"][:C, :].astype(BF16), p["VU"])
        for p in group:
            p["dS"] = lax.dot_general(p["VU"], p["KB"], (((0,), (0,)), ((), ())),
                                      preferred_element_type=F32)
        yield
        for p_even, p_odd in zip(group[0::2], group[1::2]):
            q = p_even["pair"]
            state_ref[q] = pair_state[q] * p_even["gam_c"] + jnp.where(
                lax.broadcasted_iota(jnp.int32, (N, LANES), 1) < N, p_even["dS"], p_odd["dS"])
        for p in group:
            y_ref[p["b"], :, p["sl"]] = p["Y"]

    per_group = len(probs) // RWKV_GROUPS
    waiting = [solve(probs[g * per_group:(g + 1) * per_group]) for g in range(RWKV_GROUPS)]
    running = []
    rounds = 0
    while waiting or running:
        if waiting and rounds % RWKV_SKEW == 0:
            running.append(waiting.pop(0))
        for gen in list(running):
            if next(gen, "done") == "done":
                running.remove(gen)
        rounds += 1

    for b in range(B):
        y = y_ref[b]
        d = y - head_sums(y) * (1.0 / N)
        var = head_sums(d * d) * (1.0 / N)
        yn = d * lax.rsqrt(var + GN_EPS) * gnw_ref[...] + gnb_ref[...]
        o_ref[b] = ((yn + bonus[b]) * gates[b]).astype(o_ref.dtype)


def rwkv7_mixer(rkv, misc, mu_rkv, mu_mi, w0, wdu, a0, wiu, wgu, k_k, k_a, r_k, gn_w, gn_b,
                *, batch, seq_len):
    T, W3 = rkv.shape
    W = W3 // 3
    C = RWKV_CHUNK
    nc = seq_len // C
    MI = 4 * LANES
    H = W // RWKV_HEAD_DIM
    rkv3 = rkv.reshape(batch, seq_len, W3)
    misc3 = misc.reshape(batch, seq_len, misc.shape[1])
    row = lambda v: v.reshape(1, -1)
    full = lambda shape: pl.BlockSpec(shape, lambda c: (0,) * len(shape))
    cur_blk = lambda c: (0, c, 0)
    prev_blk = lambda c: (0, jnp.maximum(c * (C // SUBLANES) - 1, 0), 0)
    out = pl.pallas_call(
        _rwkv_kernel,
        out_shape=jax.ShapeDtypeStruct((batch, seq_len, W), BF16),
        grid=(nc,),
        in_specs=[pl.BlockSpec((batch, C, W3), cur_blk),
                  pl.BlockSpec((batch, SUBLANES, W3), prev_blk),
                  pl.BlockSpec((batch, C, MI), cur_blk),
                  pl.BlockSpec((batch, SUBLANES, MI), prev_blk),
                  full((1, W3)), full((1, MI)),
                  full((1, W)), full((LANES, W)), full((1, W)), full((LANES, W)),
                  full((2 * LANES, W)), full((1, W)), full((1, W)), full((1, W)),
                  full((1, W)), full((1, W)), full((C, C))],
        out_specs=pl.BlockSpec((batch, C, W), cur_blk),
        scratch_shapes=[pltpu.VMEM((batch * H // 2, RWKV_HEAD_DIM, 2 * RWKV_HEAD_DIM), F32),
                        pltpu.VMEM((batch, C, W), F32)],
        compiler_params=_cparams(("arbitrary",)),
        name="rwkv7_mixer",
    )(rkv3, rkv3, misc3, misc3, row(mu_rkv), row(mu_mi), row(w0), wdu, row(a0), wiu, wgu,
      row(k_k), row(k_a), row(r_k), row(gn_w), row(gn_b), jnp.tril(jnp.ones((C, C), BF16)))
    return out.reshape(T, W)


def _pad_cols(w, n):
    return jnp.pad(w, ((0, 0), (0, n - w.shape[1])))


def _pad_rows(w, n):
    return jnp.pad(w, ((0, n - w.shape[0]), (0, 0)))


def _fox_rwkv_layer(x, h, p, g_next, *, batch, seq_len):
    D = x.shape[1]
    n_fox = FOX_HEADS
    fox_w = n_fox * FOX_HEAD_DIM
    rw_w = D - fox_w
    w_in = p["w_in"]
    fox_cols = 3 * fox_w + n_fox
    scale = FOX_HEAD_DIM ** -0.5

    w_q = w_in[:, :fox_w] * (scale * LOG2E)
    w_qkv = jnp.concatenate([w_q, w_in[:, fox_w:3 * fox_w]], axis=1).astype(BF16)
    w_f = w_in[:, 3 * fox_w:fox_cols]
    w_rkv = w_in[:, fox_cols:fox_cols + 3 * rw_w].astype(BF16)
    o = fox_cols + 3 * rw_w
    w_wd = w_in[:, o:o + DECAY_RANK]
    w_ad = w_in[:, o + DECAY_RANK:o + DECAY_RANK + ICLR_RANK]
    w_gd = w_in[:, o + DECAY_RANK + ICLR_RANK:]
    w_misc = jnp.concatenate([_pad_cols(w_wd, LANES), _pad_cols(w_ad, LANES), w_gd,
                              _pad_cols(w_f, LANES)], axis=1).astype(BF16)
    mu = p["shift_mu"]
    mu_rkv = mu[:3 * rw_w]
    mu_wd = mu[3 * rw_w:3 * rw_w + DECAY_RANK]
    mu_ad = mu[3 * rw_w + DECAY_RANK:3 * rw_w + DECAY_RANK + ICLR_RANK]
    mu_gd = mu[3 * rw_w + DECAY_RANK + ICLR_RANK:]
    pad1 = lambda v: jnp.pad(v, (0, LANES - v.shape[0]))
    mu_mi = jnp.concatenate([pad1(mu_wd), pad1(mu_ad), mu_gd])

    qkv = matmul(h, w_qkv, BF16, name="proj_qkv")
    rkv = matmul(h, w_rkv, F32, name="proj_rkv")
    misc = matmul(h, w_misc, F32, tn=5 * LANES, name="proj_misc")

    b_f = jnp.pad(p["b_f"], (0, LANES - n_fox)).reshape(1, LANES)
    ct = fox_cumsum(misc, 4, b_f, n_fox, seq_len=seq_len)
    ck = ct.reshape(n_fox, batch, seq_len).transpose(1, 0, 2).reshape(batch * n_fox, 1, seq_len)
    o_a = fox_attention(qkv, ck, batch=batch, seq_len=seq_len, n_heads=n_fox)

    o_b = rwkv7_mixer(
        rkv, misc, mu_rkv, mu_mi, p["w0"],
        _pad_rows(p["w_decay_up"], LANES).astype(BF16), p["a0"],
        _pad_rows(p["w_iclr_up"], LANES).astype(BF16), p["w_gate_up"].astype(BF16),
        p["k_k"], p["k_a"], p["r_k"].reshape(-1), p["gn_w"], p["gn_b"],
        batch=batch, seq_len=seq_len)

    w_out = p["w_out"].astype(BF16)
    return out_projection(o_a, o_b, x, w_out[:fox_w], w_out[fox_w:], g_next)


def kernel(x, mix_norm, w_in, b_f, shift_mu, w0, w_decay_up, a0, w_iclr_up, w_gate_up, k_k, k_a, r_k, gn_w, gn_b, w_out, pool_norm, w_pool, pool_scale, ffn_norm, w_ffn_up, conv_w, conv_b, w_ffn_down, final_norm):
    B, S, D = x.shape
    depth = ffn_norm.shape[0]
    xt = x.reshape(B * S, D)
    h = rmsnorm_bf16(xt, mix_norm[0])
    w_up_bf = w_ffn_up.astype(BF16)
    w_down_bf = w_ffn_down.astype(BF16)
    conv_taps = jnp.concatenate([conv_w, conv_b[:, None, :]], axis=1)
    e = 0
    o = 0
    for layer in range(depth):
        if layer % 2 == 0:
            p = dict(w_in=w_in[e], b_f=b_f[e], shift_mu=shift_mu[e], w0=w0[e],
                     w_decay_up=w_decay_up[e], a0=a0[e], w_iclr_up=w_iclr_up[e],
                     w_gate_up=w_gate_up[e], k_k=k_k[e], k_a=k_a[e], r_k=r_k[e],
                     gn_w=gn_w[e], gn_b=gn_b[e], w_out=w_out[e])
            xt, h = _fox_rwkv_layer(xt, h, p, ffn_norm[layer], batch=B, seq_len=S)
            e += 1
        else:
            xt, h = pool_mixer(xt, pool_norm[o], w_pool[o].astype(BF16), pool_scale[o],
                               ffn_norm[layer], seq_len=S)
            o += 1
        last = layer == depth - 1
        if last:
            g_next, emit = final_norm, "final"
        elif (layer + 1) % 2 == 0:
            g_next, emit = mix_norm[e], "x+h"
        else:
            g_next, emit = pool_norm[o], "x"
        res = conv_ffn(h, xt, w_up_bf, conv_taps, w_down_bf, g_next, layer=layer,
                       seq_len=S, emit=emit)
        if emit == "x+h":
            xt, h = res
        else:
            xt, h = res[0], None
    return xt.reshape(B, S, D)
```

```python
import functools

import jax
import jax.numpy as jnp
from jax import lax
from jax.experimental import pallas as pl
from jax.experimental.pallas import tpu as pltpu

F32 = jnp.float32
BF16 = jnp.bfloat16

RMS_EPS = 1e-6
GN_EPS = 64e-5
FOX_HEADS = 8
FOX_HEAD_DIM = 128
FOX_HEADS_PER_STEP = 2
RWKV_HEAD_DIM = 64
POOL_WINDOWS = (2, 4, 8, 16)
CONV_WIDTH = 3
DECAY_RANK = 96
ICLR_RANK = 96
GATE_RANK = 256

LANES = 128
SUBLANES = 8
VMEM_LIMIT = 56 * 1024 * 1024

RWKV_CHUNK = 64
RWKV_SKEW = 2
RWKV_GROUPS = 2
POOL_HALO = 16
CONV_HALO = 8
FFN_TILE = 512
NEG_BIG = -1e30
LOG2E = 1.4426950408889634
DECAY_SCALE = 0.6065306597126334


def _cparams(sem):
    return pltpu.CompilerParams(dimension_semantics=sem, vmem_limit_bytes=VMEM_LIMIT)


def _rms(x, g):
    ms = jnp.mean(x * x, axis=-1, keepdims=True)
    return x * lax.rsqrt(ms + RMS_EPS) * g


def _dot(a, b):
    return jnp.dot(a, b, preferred_element_type=F32)


def _dot_nt(a, b):
    return lax.dot_general(a, b, (((1,), (1,)), ((), ())), preferred_element_type=F32)


def _split3(x):
    hi = x.astype(BF16)
    r1 = x - hi.astype(F32)
    mid = r1.astype(BF16)
    lo = (r1 - mid.astype(F32)).astype(BF16)
    return hi, mid, lo


def _cumsum_rows(tri, x):
    hi, mid, lo = _split3(x)
    return _dot(tri, hi) + _dot(tri, mid) + _dot(tri, lo)


def _tril_ones(n, dtype):
    r = lax.broadcasted_iota(jnp.int32, (n, n), 0)
    c = lax.broadcasted_iota(jnp.int32, (n, n), 1)
    return (c <= r).astype(dtype)


def _sigmoid(x):
    return 0.5 * jnp.tanh(0.5 * x) + 0.5


def _rms_kernel(x_ref, g_ref, o_ref):
    o_ref[...] = _rms(x_ref[...], g_ref[...]).astype(o_ref.dtype)


def rmsnorm_bf16(x, g, tm=512):
    T, D = x.shape
    return pl.pallas_call(
        _rms_kernel,
        out_shape=jax.ShapeDtypeStruct((T, D), BF16),
        grid=(T // tm,),
        in_specs=[pl.BlockSpec((tm, D), lambda i: (i, 0)),
                  pl.BlockSpec((1, D), lambda i: (0, 0))],
        out_specs=pl.BlockSpec((tm, D), lambda i: (i, 0)),
        compiler_params=_cparams(("parallel",)),
        name="rmsnorm",
    )(x, g.reshape(1, D))


def _mm_kernel(a_ref, w_ref, o_ref):
    o_ref[...] = _dot(a_ref[...], w_ref[...]).astype(o_ref.dtype)


def matmul(a, w, out_dtype, tm=1024, tn=1024, name="matmul"):
    M, K = a.shape
    N = w.shape[1]
    tm = min(tm, M)
    tn = min(tn, N)
    return pl.pallas_call(
        _mm_kernel,
        out_shape=jax.ShapeDtypeStruct((M, N), out_dtype),
        grid=(M // tm, pl.cdiv(N, tn)),
        in_specs=[pl.BlockSpec((tm, K), lambda i, j: (i, 0)),
                  pl.BlockSpec((K, tn), lambda i, j: (0, j))],
        out_specs=pl.BlockSpec((tm, tn), lambda i, j: (i, j)),
        compiler_params=_cparams(("parallel", "parallel")),
        name=name,
    )(a, w)


def _outproj_kernel(oa_ref, ob_ref, x_ref, wa_ref, wb_ref, g_ref, xo_ref, ho_ref):
    y = x_ref[...] + _dot(oa_ref[...], wa_ref[...]) + _dot(ob_ref[...], wb_ref[...])
    xo_ref[...] = y
    ho_ref[...] = _rms(y, g_ref[...]).astype(ho_ref.dtype)


def out_projection(o_a, o_b, x, w_a, w_b, g_next, tm=512):
    T, D = x.shape
    Ka, Kb = o_a.shape[1], o_b.shape[1]
    return pl.pallas_call(
        _outproj_kernel,
        out_shape=(jax.ShapeDtypeStruct((T, D), F32), jax.ShapeDtypeStruct((T, D), BF16)),
        grid=(T // tm,),
        in_specs=[pl.BlockSpec((tm, Ka), lambda i: (i, 0)),
                  pl.BlockSpec((tm, Kb), lambda i: (i, 0)),
                  pl.BlockSpec((tm, D), lambda i: (i, 0)),
                  pl.BlockSpec((Ka, D), lambda i: (0, 0)),
                  pl.BlockSpec((Kb, D), lambda i: (0, 0)),
                  pl.BlockSpec((1, D), lambda i: (0, 0))],
        out_specs=(pl.BlockSpec((tm, D), lambda i: (i, 0)),
                   pl.BlockSpec((tm, D), lambda i: (i, 0))),
        compiler_params=_cparams(("parallel",)),
        name="out_projection",
    )(o_a, o_b, x, w_a, w_b, g_next.reshape(1, D))


def _ffn_kernel(h_ref, x_ref, wg_ref, wv_ref, cwg_ref, cwv_ref, wd_ref, gn_ref, *rest,
                tiles_per_seq, emit):
    if emit == "x+h":
        xo_ref, ho_ref, acc_ref, halo_ref = rest
    else:
        xo_ref, acc_ref, halo_ref = rest
        ho_ref = None
    i = pl.program_id(0)
    j = pl.program_id(1)
    nj = pl.num_programs(1)
    tm = h_ref.shape[0]
    tf = wd_ref.shape[0]

    @pl.when(j == 0)
    def _():
        acc_ref[...] = x_ref[...]

    @pl.when((i % tiles_per_seq) == 0)
    def _():
        halo_ref[j] = jnp.zeros(halo_ref.shape[1:], F32)

    h = h_ref[...]
    u = jnp.concatenate([_dot(h, wg_ref[...]), _dot(h, wv_ref[...])], axis=1)
    prev = halo_ref[j]
    halo_ref[j] = u[tm - CONV_HALO:, :]
    cw = jnp.concatenate([cwg_ref[...], cwv_ref[...]], axis=1)
    first_rows = lax.broadcasted_iota(jnp.int32, (CONV_HALO, 2 * tf), 0)

    def shifted(k):
        top = jnp.where(first_rows < k, pltpu.roll(prev, k, 0), pltpu.roll(u[:CONV_HALO], k, 0))
        return jnp.concatenate([top, pltpu.roll(u, k, 0)[CONV_HALO:]], axis=0)

    uc = shifted(2) * cw[0:1, :] + shifted(1) * cw[1:2, :] + u * cw[2:3, :] + cw[3:4, :]
    gate = uc[:, :tf]
    val = uc[:, tf:]
    act = (gate * _sigmoid(gate) * val).astype(BF16)
    acc_ref[...] += _dot(act, wd_ref[...])

    @pl.when(j == nj - 1)
    def _():
        y = acc_ref[...]
        if emit == "final":
            xo_ref[...] = _rms(y, gn_ref[...])
        else:
            xo_ref[...] = y
            if ho_ref is not None:
                ho_ref[...] = _rms(y, gn_ref[...]).astype(ho_ref.dtype)


def conv_ffn(h, x, w_up, cw, w_down, g_next, *, layer, seq_len, emit, tm=512, tf=FFN_TILE):
    T, D = x.shape
    F = w_down.shape[1]
    nj = F // tf
    kern = functools.partial(_ffn_kernel, tiles_per_seq=seq_len // tm, emit=emit)
    out_shape = [jax.ShapeDtypeStruct((T, D), F32)]
    out_specs = [pl.BlockSpec((tm, D), lambda i, j: (i, 0))]
    if emit == "x+h":
        out_shape.append(jax.ShapeDtypeStruct((T, D), BF16))
        out_specs.append(pl.BlockSpec((tm, D), lambda i, j: (i, 0)))
    res = pl.pallas_call(
        kern,
        out_shape=tuple(out_shape),
        grid=(T // tm, nj),
        in_specs=[pl.BlockSpec((tm, D), lambda i, j: (i, 0)),
                  pl.BlockSpec((tm, D), lambda i, j: (i, 0)),
                  pl.BlockSpec((None, D, tf), lambda i, j: (layer, 0, j)),
                  pl.BlockSpec((None, D, tf), lambda i, j: (layer, 0, j + nj)),
                  pl.BlockSpec((None, 4, tf), lambda i, j: (layer, 0, j)),
                  pl.BlockSpec((None, 4, tf), lambda i, j: (layer, 0, j + nj)),
                  pl.BlockSpec((None, tf, D), lambda i, j: (layer, j, 0)),
                  pl.BlockSpec((1, D), lambda i, j: (0, 0))],
        out_specs=tuple(out_specs),
        scratch_shapes=[pltpu.VMEM((tm, D), F32),
                        pltpu.VMEM((nj, CONV_HALO, 2 * tf), F32)],
        compiler_params=_cparams(("arbitrary", "arbitrary")),
        name="conv_ffn",
    )(h, x, w_up, w_up, cw, cw, w_down, g_next.reshape(1, D))
    return res


def _pool_kernel(x_ref, gp_ref, w_ref, sc_ref, gn_ref, xo_ref, ho_ref, halo_ref,
                 *, tiles_per_seq):
    i = pl.program_id(0)
    tm, D = x_ref.shape
    G = len(POOL_WINDOWS)
    gd = D // G
    ti = i % tiles_per_seq
    x = x_ref[...]
    h = _rms(x, gp_ref[...])

    @pl.when(ti == 0)
    def _():
        halo_ref[...] = jnp.zeros_like(halo_ref)

    prev = halo_ref[...]
    halo_ref[...] = h[tm - POOL_HALO:, :]
    pos = ti * tm + lax.broadcasted_iota(jnp.int32, (tm, 1), 0)
    outs = []
    for g, win in enumerate(POOL_WINDOWS):
        hg = h[:, g * gd:(g + 1) * gd]
        s = jnp.concatenate([prev[:, g * gd:(g + 1) * gd], hg], axis=0)
        sh = 1
        while sh < win:
            s = s + pltpu.roll(s, sh, 0)
            sh *= 2
        wsum = s[POOL_HALO:, :]
        count = jnp.minimum(pos + 1, win).astype(F32)
        pooled = wsum / count - hg
        y = _dot(pooled.astype(BF16), w_ref[g])
        outs.append(x[:, g * gd:(g + 1) * gd] + y * sc_ref[:, g * gd:(g + 1) * gd])
    xn = jnp.concatenate(outs, axis=1)
    xo_ref[...] = xn
    ho_ref[...] = _rms(xn, gn_ref[...]).astype(ho_ref.dtype)


def pool_mixer(x, g_pool, w_pool, pool_scale, g_next, *, seq_len, tm=512):
    T, D = x.shape
    G, gd, _ = w_pool.shape
    kern = functools.partial(_pool_kernel, tiles_per_seq=seq_len // tm)
    return pl.pallas_call(
        kern,
        out_shape=(jax.ShapeDtypeStruct((T, D), F32), jax.ShapeDtypeStruct((T, D), BF16)),
        grid=(T // tm,),
        in_specs=[pl.BlockSpec((tm, D), lambda i: (i, 0)),
                  pl.BlockSpec((1, D), lambda i: (0, 0)),
                  pl.BlockSpec((G, gd, gd), lambda i: (0, 0, 0)),
                  pl.BlockSpec((1, D), lambda i: (0, 0)),
                  pl.BlockSpec((1, D), lambda i: (0, 0))],
        out_specs=(pl.BlockSpec((tm, D), lambda i: (i, 0)),
                   pl.BlockSpec((tm, D), lambda i: (i, 0))),
        scratch_shapes=[pltpu.VMEM((POOL_HALO, D), F32)],
        compiler_params=_cparams(("arbitrary",)),
        name="pool_mixer",
    )(x, g_pool.reshape(1, D), w_pool, pool_scale.reshape(1, D), g_next.reshape(1, D))


def _fox_cumsum_kernel(f_ref, b_ref, ct_ref, carry_ref, *, tiles_per_seq):
    i = pl.program_id(0)
    tc = f_ref.shape[0]

    @pl.when((i % tiles_per_seq) == 0)
    def _():
        carry_ref[...] = jnp.zeros_like(carry_ref)

    z = f_ref[...] + b_ref[...]
    lf = jnp.minimum(z, 0.0) - jnp.log1p(jnp.exp(-jnp.abs(z)))
    c = _cumsum_rows(_tril_ones(tc, BF16), lf) + carry_ref[...]
    carry_ref[...] = c[tc - 1:tc, :]
    ct_ref[...] = (c * LOG2E).T[:ct_ref.shape[0], :]


def fox_cumsum(misc, f_col_block, b_f_padded, n_heads, *, seq_len, tc=256):
    T = misc.shape[0]
    kern = functools.partial(_fox_cumsum_kernel, tiles_per_seq=seq_len // tc)
    return pl.pallas_call(
        kern,
        out_shape=jax.ShapeDtypeStruct((n_heads, T), F32),
        grid=(T // tc,),
        in_specs=[pl.BlockSpec((tc, LANES), lambda i: (i, f_col_block)),
                  pl.BlockSpec((1, LANES), lambda i: (0, 0))],
        out_specs=pl.BlockSpec((n_heads, tc), lambda i: (0, i)),
        scratch_shapes=[pltpu.VMEM((1, LANES), F32)],
        compiler_params=_cparams(("arbitrary",)),
        name="fox_cumsum",
    )(misc, b_f_padded)


def _fox_kernel(q_ref, k_ref, v_ref, ck_ref, o_ref, vaug_ref, m_ref, acc_ref, sa_ref, sb_ref,
                *, tk, nsplit):
    i = pl.program_id(2)
    nh = ck_ref.shape[0]
    tq = q_ref.shape[0]
    dh = q_ref.shape[1] // nh
    hq = tq // nsplit
    nslab = tk // LANES

    @pl.when(i == 0)
    def _():
        for hh in range(nh):
            vaug_ref[hh, :, :dh] = v_ref[:, hh * dh:(hh + 1) * dh]
            vaug_ref[hh, :, dh:] = jnp.ones((vaug_ref.shape[1], dh), BF16)

    m_ref[...] = jnp.full_like(m_ref, NEG_BIG)
    acc_ref[...] = jnp.zeros_like(acc_ref)

    streams = [(hh, h * hq, slice(h * hq, (h + 1) * hq), slice(hh * dh, (hh + 1) * dh))
               for hh in range(nh) for h in range(nsplit)]

    def logits(j, s_ref):
        off = pl.multiple_of(j * tk, tk)
        for hh, _, rows, cols in streams:
            kj = k_ref[pl.ds(off, tk), cols]
            s_ref[hh, rows, :] = _dot_nt(q_ref[rows, cols], kj) - ck_ref[hh, :, pl.ds(off, tk)]

    def softmax_pv(j, s_ref, masked):
        off = pl.multiple_of(j * tk, tk)
        ps, alphas = [], []
        for hh, row0, rows, _ in streams:
            s = s_ref[hh, rows, :]
            if masked:
                c = lax.broadcasted_iota(jnp.int32, (hq, tk), 1)
                r = lax.broadcasted_iota(jnp.int32, (hq, tk), 0)
                s = jnp.where(c <= r + row0, s, NEG_BIG)
            slabs = [s[:, k * LANES:(k + 1) * LANES] for k in range(nslab)]
            smax = functools.reduce(jnp.maximum, slabs)
            m_old = m_ref[hh, rows, :]
            m_new = jnp.maximum(m_old, jnp.max(smax, axis=-1, keepdims=True))
            m_ref[hh, rows, :] = m_new
            alphas.append(jnp.exp2(m_old - m_new))
            ps.append(jnp.concatenate([jnp.exp2(sl - m_new).astype(BF16) for sl in slabs], axis=1))
        for (hh, _, rows, _), p, alpha in zip(streams, ps, alphas):
            alpha2 = jnp.concatenate([alpha, alpha], axis=1)
            acc_ref[hh, rows, :] = alpha2 * acc_ref[hh, rows, :] + _dot(
                p, vaug_ref[hh, pl.ds(off, tk), :])

    logits(0, sa_ref)

    def body(jj, carry):
        j = 2 * jj
        logits(j + 1, sb_ref)
        softmax_pv(j, sa_ref, False)
        logits(j + 2, sa_ref)
        softmax_pv(j + 1, sb_ref, False)
        return carry

    lax.fori_loop(0, i // 2, body, 0)

    @pl.when(i % 2 == 1)
    def _():
        logits(i, sb_ref)
        softmax_pv(i - 1, sa_ref, False)
        softmax_pv(i, sb_ref, True)

    @pl.when(i % 2 == 0)
    def _():
        softmax_pv(i, sa_ref, True)

    for hh in range(nh):
        acc = acc_ref[hh]
        o_ref[:, hh * dh:(hh + 1) * dh] = (acc[:, :dh] / acc[:, dh:]).astype(o_ref.dtype)


def fox_attention(qkv, ck, *, batch, seq_len, n_heads, tq=512, nsplit=1, nh=FOX_HEADS_PER_STEP):
    T = qkv.shape[0]
    nq = seq_len // tq
    dh = FOX_HEAD_DIM
    ng = n_heads // nh
    kern = functools.partial(_fox_kernel, tk=tq, nsplit=nsplit)
    return pl.pallas_call(
        kern,
        out_shape=jax.ShapeDtypeStruct((T, n_heads * dh), BF16),
        grid=(batch, ng, nq),
        in_specs=[pl.BlockSpec((tq, nh * dh), lambda b, g, i: (b * nq + i, g)),
                  pl.BlockSpec((seq_len, nh * dh), lambda b, g, i: (b, ng + g)),
                  pl.BlockSpec((seq_len, nh * dh), lambda b, g, i: (b, 2 * ng + g)),
                  pl.BlockSpec((nh, 1, seq_len), lambda b, g, i: (b * ng + g, 0, 0))],
        out_specs=pl.BlockSpec((tq, nh * dh), lambda b, g, i: (b * nq + i, g)),
        scratch_shapes=[pltpu.VMEM((nh, seq_len, 2 * dh), BF16), pltpu.VMEM((nh, tq, dh), F32),
                        pltpu.VMEM((nh, tq, 2 * dh), F32),
                        pltpu.VMEM((nh, tq, tq), F32), pltpu.VMEM((nh, tq, tq), F32)],
        compiler_params=_cparams(("parallel", "parallel", "arbitrary")),
        name="fox_attention",
    )(qkv, qkv, qkv, ck)


def _rwkv_kernel(rkv_ref, rkvp_ref, mi_ref, mip_ref, mu_rkv_ref, mu_mi_ref,
                 w0_ref, wdu_ref, a0_ref, wiu_ref, wgu_ref, kk_ref, ka_ref, rk_ref,
                 gnw_ref, gnb_ref, tri_ref, o_ref, state_ref, y_ref):
    c = pl.program_id(0)
    B, C, _ = rkv_ref.shape
    W = w0_ref.shape[1]
    N = RWKV_HEAD_DIM
    H = W // N

    @pl.when(c == 0)
    def _():
        state_ref[...] = jnp.zeros_like(state_ref)

    row0 = lax.broadcasted_iota(jnp.int32, (C, 1), 0) == 0
    tri = tri_ref[...]
    lane = lax.broadcasted_iota(jnp.int32, (2 * C, LANES), 1)
    lo_half = lane < N
    hi_half = lane >= N
    lane_c = lax.broadcasted_iota(jnp.int32, (C, LANES), 1)
    lo_half_c = lane_c < N
    hi_half_c = lane_c >= N

    def head_sums(x):
        out = []
        for k in range(W // LANES):
            xs = x[:, k * LANES:(k + 1) * LANES]
            lo = jnp.sum(jnp.where(lo_half_c, xs, 0.0), axis=-1, keepdims=True)
            hi = jnp.sum(jnp.where(hi_half_c, xs, 0.0), axis=-1, keepdims=True)
            out.append(jnp.where(lo_half_c, jnp.broadcast_to(lo, xs.shape),
                                 jnp.broadcast_to(hi, xs.shape)))
        return jnp.concatenate(out, axis=1)

    def token_shift_lerp(cur, prev8, mu_ref):
        last = jnp.where(c == 0, 0.0, prev8[SUBLANES - 1:SUBLANES, :])
        shifted = jnp.where(row0, last, pltpu.roll(cur, 1, 0))
        return cur + (shifted - cur) * mu_ref[...]

    r2 = lax.broadcasted_iota(jnp.int32, (2 * C, 2 * C), 0)
    c2 = lax.broadcasted_iota(jnp.int32, (2 * C, 2 * C), 1)
    rr = jnp.where(r2 >= C, r2 - C, r2)
    cc = jnp.where(c2 >= C, c2 - C, c2)
    amask = cc + jnp.where(r2 >= C, 1, 0) <= rr
    ri = lax.broadcasted_iota(jnp.int32, (C, C), 0)
    ci = lax.broadcasted_iota(jnp.int32, (C, C), 1)
    eye = (ri == ci).astype(F32)

    probs = []
    gates = []
    bonus = []
    for b in range(B):
        rw = token_shift_lerp(rkv_ref[b], rkvp_ref[b], mu_rkv_ref)
        mi = token_shift_lerp(mi_ref[b], mip_ref[b], mu_mi_ref)
        r = rw[:, :W]
        kb = rw[:, W:2 * W]
        vb = rw[:, 2 * W:]
        wd = mi[:, :LANES]
        ad = mi[:, LANES:2 * LANES]
        gd = mi[:, 2 * LANES:]

        lw = -DECAY_SCALE * _sigmoid(
            w0_ref[...] + _dot(jnp.tanh(wd).astype(BF16), wdu_ref[...]))
        a = _sigmoid(a0_ref[...] + _dot(ad.astype(BF16), wiu_ref[...]))
        gates.append(_dot(_sigmoid(gd).astype(BF16), wgu_ref[...]))
        kkraw = kb * kk_ref[...]
        kmod = kb * (1.0 + (a - 1.0) * ka_ref[...])
        kk = kkraw / jnp.maximum(jnp.sqrt(head_sums(kkraw * kkraw)), 1e-12)
        bk = a * kk
        bonus.append(head_sums(r * kmod * rk_ref[...]) * vb)

        G = _cumsum_rows(tri, lw)
        gam = jnp.exp(G)
        gam_inv = jnp.exp(-G)
        gam_c = gam[C - 1:C, :]
        gam_tail = jnp.exp(G[C - 1:C, :] - G)
        L1 = jnp.concatenate([r * gam, kk * jnp.exp(G - lw)], axis=0)
        R1 = jnp.concatenate([bk * gam_inv, kmod * gam_inv], axis=0).astype(BF16)
        KB = jnp.concatenate([bk * gam_tail, kmod * gam_tail], axis=0).astype(BF16)

        for hd in range(H):
            slab = slice((hd // 2) * LANES, (hd // 2 + 1) * LANES)
            mine = lo_half if hd % 2 == 0 else hi_half
            probs.append(dict(
                b=b, hd=hd, sl=slice(hd * N, (hd + 1) * N), slab=slab, mine=mine,
                pair=b * (H // 2) + hd // 2, v=vb[:, hd * N:(hd + 1) * N],
                L1=jnp.where(mine, L1[:, slab], 0.0).astype(BF16),
                R1=R1[:, slab], KB=KB[:, slab], gam_c=gam_c[:, slab]))

    def solve(group):
        pair_state = {p["pair"]: state_ref[p["pair"]] for p in group[0::2]}
        pair_state_bf = {q: s.astype(BF16) for q, s in pair_state.items()}
        for p in group:
            p["A"] = jnp.where(amask, _dot_nt(p["L1"], p["R1"]), 0.0)
        yield
        for p in group:
            p["RH"] = _dot_nt(p["L1"], pair_state_bf[p["pair"]])
        for p in group:
            nm = p["A"][C:, :C]
            p["X"] = eye - nm
            nmb = nm.astype(BF16)
            p["P"] = _dot(nmb, nmb)
        yield
        span = 2
        while span < C:
            for p in group:
                p["Pb"] = p["P"].astype(BF16)
                p["X"] = p["X"] + _dot(p["X"].astype(BF16), p["Pb"])
            span *= 2
            if span < C:
                for p in group:
                    p["P"] = _dot(p["Pb"], p["Pb"])
            yield
        for p in group:
            zv = jnp.concatenate([jnp.zeros_like(p["v"]), p["v"]], axis=0).astype(BF16)
            p["Z"] = p["RH"][C:, :] + _dot(p["A"][C:, :].astype(BF16), zv)
        yield
        for p in group:
            p["U"] = _dot(p["X"].astype(BF16), p["Z"].astype(BF16))
        yield
        for p in group:
            p["VU"] = jnp.concatenate([-p["U"], p["v"]], axis=0).astype(BF16)
            p["Y"] = p["RH"][:C, :] + _dot(p["A"][:C, :].astype(BF16), p["VU"])
        for p in group:
            p["dS"] = lax.dot_general(p["VU"], p["KB"], (((0,), (0,)), ((), ())),
                                      preferred_element_type=F32)
        yield
        for p_even, p_odd in zip(group[0::2], group[1::2]):
            q = p_even["pair"]
            state_ref[q] = pair_state[q] * p_even["gam_c"] + jnp.where(
                lax.broadcasted_iota(jnp.int32, (N, LANES), 1) < N, p_even["dS"], p_odd["dS"])
        for p in group:
            y_ref[p["b"], :, p["sl"]] = p["Y"]

    per_group = len(probs) // RWKV_GROUPS
    waiting = [solve(probs[g * per_group:(g + 1) * per_group]) for g in range(RWKV_GROUPS)]
    running = []
    rounds = 0
    while waiting or running:
        if waiting and rounds % RWKV_SKEW == 0:
            running.append(waiting.pop(0))
        for gen in list(running):
            if next(gen, "done") == "done":
                running.remove(gen)
        rounds += 1

    for b in range(B):
        y = y_ref[b]
        d = y - head_sums(y) * (1.0 / N)
        var = head_sums(d * d) * (1.0 / N)
        yn = d * lax.rsqrt(var + GN_EPS) * gnw_ref[...] + gnb_ref[...]
        o_ref[b] = ((yn + bonus[b]) * gates[b]).astype(o_ref.dtype)


def rwkv7_mixer(rkv, misc, mu_rkv, mu_mi, w0, wdu, a0, wiu, wgu, k_k, k_a, r_k, gn_w, gn_b,
                *, batch, seq_len):
    T, W3 = rkv.shape
    W = W3 // 3
    C = RWKV_CHUNK
    nc = seq_len // C
    MI = 4 * LANES
    H = W // RWKV_HEAD_DIM
    rkv3 = rkv.reshape(batch, seq_len, W3)
    misc3 = misc.reshape(batch, seq_len, misc.shape[1])
    row = lambda v: v.reshape(1, -1)
    full = lambda shape: pl.BlockSpec(shape, lambda c: (0,) * len(shape))
    cur_blk = lambda c: (0, c, 0)
    prev_blk = lambda c: (0, jnp.maximum(c * (C // SUBLANES) - 1, 0), 0)
    out = pl.pallas_call(
        _rwkv_kernel,
        out_shape=jax.ShapeDtypeStruct((batch, seq_len, W), BF16),
        grid=(nc,),
        in_specs=[pl.BlockSpec((batch, C, W3), cur_blk),
                  pl.BlockSpec((batch, SUBLANES, W3), prev_blk),
                  pl.BlockSpec((batch, C, MI), cur_blk),
                  pl.BlockSpec((batch, SUBLANES, MI), prev_blk),
                  full((1, W3)), full((1, MI)),
                  full((1, W)), full((LANES, W)), full((1, W)), full((LANES, W)),
                  full((2 * LANES, W)), full((1, W)), full((1, W)), full((1, W)),
                  full((1, W)), full((1, W)), full((C, C))],
        out_specs=pl.BlockSpec((batch, C, W), cur_blk),
        scratch_shapes=[pltpu.VMEM((batch * H // 2, RWKV_HEAD_DIM, 2 * RWKV_HEAD_DIM), F32),
                        pltpu.VMEM((batch, C, W), F32)],
        compiler_params=_cparams(("arbitrary",)),
        name="rwkv7_mixer",
    )(rkv3, rkv3, misc3, misc3, row(mu_rkv), row(mu_mi), row(w0), wdu, row(a0), wiu, wgu,
      row(k_k), row(k_a), row(r_k), row(gn_w), row(gn_b), jnp.tril(jnp.ones((C, C), BF16)))
    return out.reshape(T, W)


def _pad_cols(w, n):
    return jnp.pad(w, ((0, 0), (0, n - w.shape[1])))


def _pad_rows(w, n):
    return jnp.pad(w, ((0, n - w.shape[0]), (0, 0)))


def _fox_rwkv_layer(x, h, p, g_next, *, batch, seq_len):
    D = x.shape[1]
    n_fox = FOX_HEADS
    fox_w = n_fox * FOX_HEAD_DIM
    rw_w = D - fox_w
    w_in = p["w_in"]
    fox_cols = 3 * fox_w + n_fox
    scale = FOX_HEAD_DIM ** -0.5

    w_q = w_in[:, :fox_w] * (scale * LOG2E)
    w_qkv = jnp.concatenate([w_q, w_in[:, fox_w:3 * fox_w]], axis=1).astype(BF16)
    w_f = w_in[:, 3 * fox_w:fox_cols]
    w_rkv = w_in[:, fox_cols:fox_cols + 3 * rw_w].astype(BF16)
    o = fox_cols + 3 * rw_w
    w_wd = w_in[:, o:o + DECAY_RANK]
    w_ad = w_in[:, o + DECAY_RANK:o + DECAY_RANK + ICLR_RANK]
    w_gd = w_in[:, o + DECAY_RANK + ICLR_RANK:]
    w_misc = jnp.concatenate([_pad_cols(w_wd, LANES), _pad_cols(w_ad, LANES), w_gd,
                              _pad_cols(w_f, LANES)], axis=1).astype(BF16)
    mu = p["shift_mu"]
    mu_rkv = mu[:3 * rw_w]
    mu_wd = mu[3 * rw_w:3 * rw_w + DECAY_RANK]
    mu_ad = mu[3 * rw_w + DECAY_RANK:3 * rw_w + DECAY_RANK + ICLR_RANK]
    mu_gd = mu[3 * rw_w + DECAY_RANK + ICLR_RANK:]
    pad1 = lambda v: jnp.pad(v, (0, LANES - v.shape[0]))
    mu_mi = jnp.concatenate([pad1(mu_wd), pad1(mu_ad), mu_gd])

    qkv = matmul(h, w_qkv, BF16, name="proj_qkv")
    rkv = matmul(h, w_rkv, F32, name="proj_rkv")
    misc = matmul(h, w_misc, F32, tn=5 * LANES, name="proj_misc")

    b_f = jnp.pad(p["b_f"], (0, LANES - n_fox)).reshape(1, LANES)
    ct = fox_cumsum(misc, 4, b_f, n_fox, seq_len=seq_len)
    ck = ct.reshape(n_fox, batch, seq_len).transpose(1, 0, 2).reshape(batch * n_fox, 1, seq_len)
    o_a = fox_attention(qkv, ck, batch=batch, seq_len=seq_len, n_heads=n_fox)

    o_b = rwkv7_mixer(
        rkv, misc, mu_rkv, mu_mi, p["w0"],
        _pad_rows(p["w_decay_up"], LANES).astype(BF16), p["a0"],
        _pad_rows(p["w_iclr_up"], LANES).astype(BF16), p["w_gate_up"].astype(BF16),
        p["k_k"], p["k_a"], p["r_k"].reshape(-1), p["gn_w"], p["gn_b"],
        batch=batch, seq_len=seq_len)

    w_out = p["w_out"].astype(BF16)
    return out_projection(o_a, o_b, x, w_out[:fox_w], w_out[fox_w:], g_next)


def kernel(x, mix_norm, w_in, b_f, shift_mu, w0, w_decay_up, a0, w_iclr_up, w_gate_up, k_k, k_a, r_k, gn_w, gn_b, w_out, pool_norm, w_pool, pool_scale, ffn_norm, w_ffn_up, conv_w, conv_b, w_ffn_down, final_norm):
    B, S, D = x.shape
    depth = ffn_norm.shape[0]
    xt = x.reshape(B * S, D)
    h = rmsnorm_bf16(xt, mix_norm[0])
    w_up_bf = w_ffn_up.astype(BF16)
    w_down_bf = w_ffn_down.astype(BF16)
    conv_taps = jnp.concatenate([conv_w, conv_b[:, None, :]], axis=1)
    e = 0
    o = 0
    for layer in range(depth):
        if layer % 2 == 0:
            p = dict(w_in=w_in[e], b_f=b_f[e], shift_mu=shift_mu[e], w0=w0[e],
                     w_decay_up=w_decay_up[e], a0=a0[e], w_iclr_up=w_iclr_up[e],
                     w_gate_up=w_gate_up[e], k_k=k_k[e], k_a=k_a[e], r_k=r_k[e],
                     gn_w=gn_w[e], gn_b=gn_b[e], w_out=w_out[e])
            xt, h = _fox_rwkv_layer(xt, h, p, ffn_norm[layer], batch=B, seq_len=S)
            e += 1
        else:
            xt, h = pool_mixer(xt, pool_norm[o], w_pool[o].astype(BF16), pool_scale[o],
                               ffn_norm[layer], seq_len=S)
            o += 1
        last = layer == depth - 1
        if last:
            g_next, emit = final_norm, "final"
        elif (layer + 1) % 2 == 0:
            g_next, emit = mix_norm[e], "x+h"
        else:
            g_next, emit = pool_norm[o], "x"
        res = conv_ffn(h, xt, w_up_bf, conv_taps, w_down_bf, g_next, layer=layer,
                       seq_len=S, emit=emit)
        if emit == "x+h":
            xt, h = res
        else:
            xt, h = res[0], None
    return xt.reshape(B, S, D)
```
